```python
import math
import jax
import jax.numpy as jnp
from jax import lax
import numpy as np


D_MODEL = 4096
BATCH = 8
SEQ = 2048
DEPTH = 4

CTX_LEN = 256
GRID_W = 64

GDN_HEADS = 12
GDN_DK = 128
GDN_DV = 128
GDN_CONV = 5
GDN_CHUNK = 64
MLA_HEADS = 12
MLA_Q_RANK = 1024
MLA_KV_RANK = 512
MLA_NOPE = 128
MLA_ROPE = 64
MLA_V = 128
MLA_SCALE = (MLA_NOPE + MLA_ROPE) ** -0.5
ATTN_BLOCK = 128
ROPE_THETA = 10000.0
S5_WIDTH = 1024
S5_GROUP = 16
S5_GROUPS = S5_WIDTH // S5_GROUP
S5_STATE = 64

GDN_QK = GDN_HEADS * GDN_DK
GDN_VW = GDN_HEADS * GDN_DV
GDN_NQKV = 2 * GDN_QK + GDN_VW
N_GDN_IN = GDN_NQKV + GDN_VW + 4 * GDN_HEADS
N_MLA_IN = MLA_Q_RANK + MLA_KV_RANK + MLA_ROPE
N_S5_IN = S5_WIDTH
N_IN = N_GDN_IN + N_MLA_IN + N_S5_IN
MIX_WIDTH = GDN_VW + MLA_HEADS * MLA_V + S5_WIDTH

ADA_RANK = 256
N_MOD = 6
N_EXPERTS = 32
TOP_K = 4
EXPERT_FF = 384
SWIGLU_LIMIT = 7.0
SWIGLU_ALPHA = 1.702
RMS_EPS = 1e-6
F32 = jnp.float32

kernel_name = 'hybrid_gdn_mla_s5_moe_diffusion_trunk'


def _rms_norm(x, g):
    xf = x.astype(F32)
    y = xf * lax.rsqrt(jnp.mean(xf * xf, axis=-1, keepdims=True) + RMS_EPS)
    return (y * g.astype(F32)).astype(x.dtype)


def _l2_norm(x):
    xf = x.astype(F32)
    return xf * lax.rsqrt(jnp.sum(xf * xf, axis=-1, keepdims=True) + RMS_EPS)


def _modulate(h, shift, scale):
    return h * (1.0 + scale) + shift


def _ada_mod(cvec, lp):
    m = (jax.nn.silu(cvec) @ lp['ada_dn']) @ lp['ada_up'] + lp['ada_b']
    return jnp.split(m, N_MOD, axis=-1)


def _dwconv_centred(x, w):
    k = w.shape[0]
    return lax.conv_general_dilated(x, w[:, None, :].astype(x.dtype), window_strides=(1,),
                                    padding=[(k // 2, k // 2)],
                                    dimension_numbers=('NWC', 'WIO', 'NWC'),
                                    feature_group_count=x.shape[-1])


def _axial_rope(n_tokens):
    n_rows = n_tokens // GRID_W
    row = jnp.repeat(jnp.arange(n_rows), GRID_W).astype(F32)
    col = jnp.tile(jnp.arange(GRID_W), n_rows).astype(F32)
    axis_dim = MLA_ROPE // 2
    inv = ROPE_THETA ** (-jnp.arange(0, axis_dim, 2, dtype=F32) / axis_dim)
    ang = jnp.concatenate([row[:, None] * inv, col[:, None] * inv], axis=-1)
    return jnp.cos(ang), jnp.sin(ang)


def _rope(t, cos, sin):
    half = t.shape[-1] // 2
    t1, t2 = t[..., :half], t[..., half:]
    cos = cos.astype(t.dtype)
    sin = sin.astype(t.dtype)
    return jnp.concatenate([t1 * cos - t2 * sin, t1 * sin + t2 * cos], axis=-1)


def _gated_delta_chunked(q, k, v, g, beta, s0, with_out):
    b, h, l, dk = q.shape
    dv = v.shape[-1]
    cs = GDN_CHUNK
    n = l // cs
    q = (q * dk ** -0.5).reshape(b, h, n, cs, dk)
    k = k.reshape(b, h, n, cs, dk)
    v = v.reshape(b, h, n, cs, dv)
    beta = beta.reshape(b, h, n, cs)
    g = jnp.cumsum(g.reshape(b, h, n, cs), axis=-1)
    idx = jnp.arange(cs)
    incl = idx[:, None] >= idx[None, :]
    strict = idx[:, None] > idx[None, :]
    decay = jnp.exp(jnp.where(incl, g[..., :, None] - g[..., None, :], -jnp.inf))
    kb = k * beta[..., None]
    a = jnp.where(strict, jnp.einsum('bhncd,bhnsd->bhncs', kb, k) * decay, 0.0)
    rhs = jnp.concatenate([v * beta[..., None], kb * jnp.exp(g)[..., None]], axis=-1)
    uw = lax.linalg.triangular_solve(a, rhs, left_side=True, lower=True, unit_diagonal=True)
    u, w = uw[..., :dv], uw[..., dv:]

    def step(s, inp):
        qi, ki, ui, wi, gi, di = inp
        v_new = ui - jnp.einsum('bhcd,bhde->bhce', wi, s)
        g_last = gi[..., -1]
        s_new = s * jnp.exp(g_last)[..., None, None] + jnp.einsum(
            'bhcd,bhce->bhde', ki * jnp.exp(g_last[..., None] - gi)[..., None], v_new)
        if not with_out:
            return s_new, None
        o = jnp.einsum('bhcd,bhde->bhce', qi * jnp.exp(gi)[..., None], s) + jnp.einsum(
            'bhcs,bhse->bhce', jnp.einsum('bhcd,bhsd->bhcs', qi, ki) * di, v_new)
        return s_new, o

    xs = tuple(jnp.moveaxis(t, 2, 0) for t in (q, k, u, w, g, decay))
    s_final, o = lax.scan(step, s0, xs)
    if with_out:
        o = jnp.moveaxis(o, 0, 2).reshape(b, h, l, dv)
    return o, s_final


def _gdn_stream(p, lp):
    b, l, _ = p.shape
    qkv = jax.nn.silu(_dwconv_centred(p[..., :GDN_NQKV], lp['gdn_conv']))
    heads = lambda t, d: jnp.transpose(t.reshape(b, l, GDN_HEADS, d), (0, 2, 1, 3))
    q = _l2_norm(heads(qkv[..., :GDN_QK], GDN_DK))
    k = _l2_norm(heads(qkv[..., GDN_QK:2 * GDN_QK], GDN_DK))
    v = heads(qkv[..., 2 * GDN_QK:], GDN_DV).astype(F32)
    z = p[..., GDN_NQKV:GDN_NQKV + GDN_VW].reshape(b, l, GDN_HEADS, GDN_DV)
    o = GDN_NQKV + GDN_VW
    beta_logit = p[..., o:o + 2 * GDN_HEADS].astype(F32).reshape(b, l, 2, GDN_HEADS)
    a_logit = p[..., o + 2 * GDN_HEADS:].astype(F32).reshape(b, l, 2, GDN_HEADS)
    beta = jnp.transpose(jax.nn.sigmoid(beta_logit), (2, 0, 3, 1))
    g = -jnp.exp(lp['gdn_a_log'].astype(F32))[:, None, :, None] * jnp.transpose(
        jax.nn.softplus(a_logit + lp['gdn_dt_bias'].astype(F32)), (2, 0, 3, 1))
    return q, k, v, z, beta, g


def _gdn_out(o, z, gain):
    b, h, l, dv = o.shape
    o = _rms_norm(jnp.transpose(o, (0, 2, 1, 3)), gain).astype(z.dtype) * jax.nn.silu(z)
    return o.reshape(b, l, h * dv)


def _gdn_mixer(pc, pl, lp, ctx_out):
    qc, kc, vc, zc, bc, gc = _gdn_stream(pc, lp)
    ql, kl, vl, zl, bl, gl = _gdn_stream(pl, lp)
    flip = lambda t: jnp.flip(t, axis=2)
    s0 = jnp.zeros((pl.shape[0], GDN_HEADS, GDN_DK, GDN_DV), F32)
    oc_f, sc_f = _gated_delta_chunked(qc, kc, vc, gc[0], bc[0], s0, ctx_out)
    ol_f, _ = _gated_delta_chunked(ql, kl, vl, gl[0], bl[0], sc_f, True)
    oc_b, sc_b = _gated_delta_chunked(flip(qc), flip(kc), flip(vc), flip(gc[1]), flip(bc[1]), s0, ctx_out)
    ol_b, _ = _gated_delta_chunked(flip(ql), flip(kl), flip(vl), flip(gl[1]), flip(bl[1]), sc_b, True)
    out_l = _gdn_out(ol_f + flip(ol_b), zl, lp['gdn_norm'])
    out_c = _gdn_out(oc_f + flip(oc_b), zc, lp['gdn_norm']) if ctx_out else None
    return out_c, out_l


def _mla_stream(p, lp, cos, sin):
    b, l, _ = p.shape
    cq = _rms_norm(p[..., :MLA_Q_RANK], lp['mla_q_norm'])
    ckv = _rms_norm(p[..., MLA_Q_RANK:MLA_Q_RANK + MLA_KV_RANK], lp['mla_kv_norm'])
    kr = _rms_norm(p[..., MLA_Q_RANK + MLA_KV_RANK:], lp['mla_kr_norm'])
    q = (cq @ lp['mla_w_uq']).reshape(b, l, MLA_HEADS, MLA_NOPE + MLA_ROPE)
    kv = (ckv @ lp['mla_w_ukv']).reshape(b, l, MLA_HEADS, MLA_NOPE + MLA_V)
    qn = _rms_norm(q[..., :MLA_NOPE], lp['mla_qn_norm'])
    qr = _rms_norm(q[..., MLA_NOPE:], lp['mla_qr_norm'])
    kn = _rms_norm(kv[..., :MLA_NOPE], lp['mla_kn_norm'])
    v = kv[..., MLA_NOPE:]
    if cos is not None:
        qr = _rope(qr, cos[:, None, :], sin[:, None, :])
        kr = _rope(kr, cos, sin)
    return qn, qr, kn, kr, v


def _attend(qn, qr, kn, kr, v):
    s = (jnp.einsum('bqhd,bkhd->bhqk', qn, kn, preferred_element_type=F32)
         + jnp.einsum('bqhd,bkd->bhqk', qr, kr, preferred_element_type=F32)) * MLA_SCALE
    p = jax.nn.softmax(s, axis=-1).astype(v.dtype)
    return jnp.einsum('bhqk,bkhd->bqhd', p, v)


def _mla_mixer(pc, pl, lp, cos, sin, ctx_out):
    qnc, qrc, knc, krc, vc = _mla_stream(pc, lp, None, None)
    qnl, qrl, knl, krl, vl = _mla_stream(pl, lp, cos, sin)
    b, l = pl.shape[:2]
    kn = jnp.concatenate([knc, knl], axis=1)
    kr = jnp.concatenate([krc, krl], axis=1)
    v = jnp.concatenate([vc, vl], axis=1)
    nb = l // ATTN_BLOCK
    blk = lambda t: jnp.moveaxis(t.reshape(b, nb, ATTN_BLOCK, *t.shape[2:]), 1, 0)
    ol = lax.map(lambda qs: _attend(qs[0], qs[1], kn, kr, v), (blk(qnl), blk(qrl)))
    ol = jnp.moveaxis(ol, 0, 1).reshape(b, l, MLA_HEADS * MLA_V)
    oc = _attend(qnc, qrc, knc, krc, vc).reshape(b, -1, MLA_HEADS * MLA_V) if ctx_out else None
    return oc, ol


def _s5_discretise(lam_re, lam_im, log_step, b_re, b_im):
    lam_re, lam_im = lam_re.astype(F32), lam_im.astype(F32)
    dt = jnp.exp(log_step.astype(F32))[:, None]
    mag = jnp.exp(lam_re * dt)
    ang = lam_im * dt
    a_re, a_im = mag * jnp.cos(ang), mag * jnp.sin(ang)
    den = lam_re * lam_re + lam_im * lam_im
    nr, ni = a_re - 1.0, a_im
    coef_re = ((nr * lam_re + ni * lam_im) / den)[..., None]
    coef_im = ((ni * lam_re - nr * lam_im) / den)[..., None]
    b_re, b_im = b_re.astype(F32), b_im.astype(F32)
    return a_re, a_im, coef_re * b_re - coef_im * b_im, coef_re * b_im + coef_im * b_re


def _s5_scan(u, a_re, a_im, bb_re, bb_im, h0_re, h0_im):
    bu_re = jnp.einsum('blgh,gph->blgp', u, bb_re)
    bu_im = jnp.einsum('blgh,gph->blgp', u, bb_im)
    bu_re = bu_re.at[:, 0].add(a_re * h0_re - a_im * h0_im)
    bu_im = bu_im.at[:, 0].add(a_re * h0_im + a_im * h0_re)
    ar = jnp.broadcast_to(a_re, bu_re.shape)
    ai = jnp.broadcast_to(a_im, bu_im.shape)

    def combine(e1, e2):
        a1r, a1i, b1r, b1i = e1
        a2r, a2i, b2r, b2i = e2
        return (a2r * a1r - a2i * a1i, a2r * a1i + a2i * a1r,
                a2r * b1r - a2i * b1i + b2r, a2r * b1i + a2i * b1r + b2i)

    _, _, h_re, h_im = lax.associative_scan(combine, (ar, ai, bu_re, bu_im), axis=1)
    return h_re, h_im


def _s5_readout(h_re, h_im, c_re, c_im):
    return jnp.einsum('blgp,ghp->blgh', h_re, c_re) - jnp.einsum('blgp,ghp->blgh', h_im, c_im)


def _s5_glu(y, u, lp):
    b, l = u.shape[:2]
    y = (y.reshape(b, l, S5_WIDTH) + lp['s5_d'].astype(F32) * u.astype(F32)).astype(u.dtype)
    z = jax.nn.gelu(y) @ lp['s5_w_glu'] + lp['s5_b_glu']
    return z[..., :S5_WIDTH] * jax.nn.sigmoid(z[..., S5_WIDTH:])


def _s5_mixer(uc, ul, lp, ctx_out):
    b, n_ctx, _ = uc.shape
    n_lat = ul.shape[1]
    ucg = uc.astype(F32).reshape(b, n_ctx, S5_GROUPS, S5_GROUP)
    ulg = ul.astype(F32).reshape(b, n_lat, S5_GROUPS, S5_GROUP)
    zero = jnp.zeros((b, S5_GROUPS, S5_STATE), F32)
    ys_c, ys_l = [], []
    for d in range(2):
        a_re, a_im, bb_re, bb_im = _s5_discretise(lp['s5_lam_re'][d], lp['s5_lam_im'][d],
                                                  lp['s5_log_step'][d], lp['s5_b_re'][d], lp['s5_b_im'][d])
        c_re = lp['s5_c_re'][d].astype(F32)
        c_im = lp['s5_c_im'][d].astype(F32)
        orient = (lambda t: t) if d == 0 else (lambda t: jnp.flip(t, axis=1))
        hc_re, hc_im = _s5_scan(orient(ucg), a_re, a_im, bb_re, bb_im, zero, zero)
        hl_re, hl_im = _s5_scan(orient(ulg), a_re, a_im, bb_re, bb_im, hc_re[:, -1], hc_im[:, -1])
        ys_l.append(orient(_s5_readout(hl_re, hl_im, c_re, c_im)))
        if ctx_out:
            ys_c.append(orient(_s5_readout(hc_re, hc_im, c_re, c_im)))
    out_l = _s5_glu(ys_l[0] + ys_l[1], ul, lp)
    out_c = _s5_glu(ys_c[0] + ys_c[1], uc, lp) if ctx_out else None
    return out_c, out_l


def _moe(h, lp):
    shp = h.shape
    t = h.reshape(-1, shp[-1])
    logits = (t @ lp['moe_w_router'] + lp['moe_b_router']).astype(F32)
    top_v, top_i = lax.top_k(logits, TOP_K)
    w = jax.nn.softmax(top_v, axis=-1)
    gates = jnp.einsum('tk,tke->te', w, jax.nn.one_hot(top_i, N_EXPERTS, dtype=F32))
    out = jnp.zeros(t.shape, F32)
    for e in range(N_EXPERTS):
        gu = t @ lp['moe_w_gu'][e] + lp['moe_b_gu'][e]
        x_glu = jnp.minimum(gu[:, 0::2], SWIGLU_LIMIT)
        x_lin = jnp.clip(gu[:, 1::2], -SWIGLU_LIMIT, SWIGLU_LIMIT)
        y = ((x_lin + 1.0) * x_glu * jax.nn.sigmoid(SWIGLU_ALPHA * x_glu)) @ lp['moe_w_dn'][e] + lp['moe_b_dn'][e]
        out = out + gates[:, e:e + 1] * y
    return out.astype(h.dtype).reshape(shp)


def _layer(x, xc, c, c_ctx, lp, cos, sin, last):
    sh1, sc1, g1, sh2, sc2, g2 = [m[:, None, :] for m in _ada_mod(c, lp)]
    sh1c, sc1c, g1c, sh2c, sc2c, g2c = _ada_mod(c_ctx, lp)
    hl = _modulate(_rms_norm(x, lp['norm1']), sh1, sc1)
    hc = _modulate(_rms_norm(xc, lp['norm1']), sh1c, sc1c)
    pl = hl @ lp['w_in']
    pc = hc @ lp['w_in']
    e_gdn, e_mla = N_GDN_IN, N_GDN_IN + N_MLA_IN
    ctx_out = not last
    gdn_c, gdn_l = _gdn_mixer(pc[..., :e_gdn], pl[..., :e_gdn], lp, ctx_out)
    mla_c, mla_l = _mla_mixer(pc[..., e_gdn:e_mla], pl[..., e_gdn:e_mla], lp, cos, sin, ctx_out)
    s5_c, s5_l = _s5_mixer(pc[..., e_mla:], pl[..., e_mla:], lp, ctx_out)
    x = x + g1 * (jnp.concatenate([gdn_l, mla_l, s5_l], axis=-1) @ lp['w_out'])
    hl2 = _modulate(_rms_norm(x, lp['norm2']), sh2, sc2)
    if last:
        return x + g2 * _moe(hl2, lp), xc
    xc = xc + g1c * (jnp.concatenate([gdn_c, mla_c, s5_c], axis=-1) @ lp['w_out'])
    hc2 = _modulate(_rms_norm(xc, lp['norm2']), sh2c, sc2c)
    n_ctx = xc.shape[1]
    y = _moe(jnp.concatenate([hc2, hl2], axis=1), lp)
    return x + g2 * y[:, n_ctx:], xc + g2c * y[:, :n_ctx]


def setup_inputs(seed: int = 0) -> dict:
    key = jax.random.key(seed)
    ks = iter(jax.random.split(key, 48))
    L, D = DEPTH, D_MODEL

    def nrm(shape, scale):
        return jax.random.normal(next(ks), shape, F32) * scale

    def gain(shape):
        return 1.0 + 0.05 * jax.random.normal(next(ks), shape, F32)

    x = nrm((BATCH, SEQ, D), 1.0)
    c = nrm((BATCH, D), 1.0)
    ctx = nrm((BATCH, CTX_LEN, D), 1.0)
    c_ctx = nrm((D,), 1.0)
    ada_dn = nrm((L, D, ADA_RANK), D ** -0.5)
    ada_up = nrm((L, ADA_RANK, N_MOD * D), 0.5 * ADA_RANK ** -0.5)
    ada_b = nrm((L, N_MOD * D), 0.02)
    norm1 = gain((L, D))
    norm2 = gain((L, D))
    w_in = nrm((L, D, N_IN), D ** -0.5)
    gdn_conv = nrm((L, GDN_CONV, GDN_NQKV), GDN_CONV ** -0.5)
    gdn_a_log = jnp.log(jax.random.uniform(next(ks), (L, 2, GDN_HEADS), F32, 1.0, 16.0))
    dt = jnp.exp(jax.random.uniform(next(ks), (L, 2, GDN_HEADS), F32, math.log(1e-3), math.log(1e-1)))
    gdn_dt_bias = dt + jnp.log(-jnp.expm1(-dt))
    gdn_norm = gain((L, GDN_DV))
    mla_q_norm = gain((L, MLA_Q_RANK))
    mla_kv_norm = gain((L, MLA_KV_RANK))
    mla_w_uq = nrm((L, MLA_Q_RANK, MLA_HEADS * (MLA_NOPE + MLA_ROPE)), MLA_Q_RANK ** -0.5)
    mla_w_ukv = nrm((L, MLA_KV_RANK, MLA_HEADS * (MLA_NOPE + MLA_V)), MLA_KV_RANK ** -0.5)
    mla_qn_norm = gain((L, MLA_NOPE))
    mla_qr_norm = gain((L, MLA_ROPE))
    mla_kn_norm = gain((L, MLA_NOPE))
    mla_kr_norm = gain((L, MLA_ROPE))
    gp = (L, 2, S5_GROUPS, S5_STATE)
    s5_lam_re = -0.5 * jnp.exp(nrm(gp, 0.05))
    s5_lam_im = math.pi * jnp.arange(S5_STATE, dtype=F32) + nrm(gp, 0.05)
    s5_log_step = jax.random.uniform(next(ks), (L, 2, S5_GROUPS), F32, math.log(1e-3), math.log(1e-1))
    s5_b_re = nrm(gp + (S5_GROUP,), (2 * S5_GROUP) ** -0.5)
    s5_b_im = nrm(gp + (S5_GROUP,), (2 * S5_GROUP) ** -0.5)
    s5_c_re = nrm((L, 2, S5_GROUPS, S5_GROUP, S5_STATE), 0.5)
    s5_c_im = nrm((L, 2, S5_GROUPS, S5_GROUP, S5_STATE), 0.5)
    s5_d = nrm((L, S5_WIDTH), 0.5)
    s5_w_glu = nrm((L, S5_WIDTH, 2 * S5_WIDTH), S5_WIDTH ** -0.5)
    s5_b_glu = nrm((L, 2 * S5_WIDTH), 0.02)
    w_out = nrm((L, MIX_WIDTH, D), MIX_WIDTH ** -0.5)
    moe_w_router = nrm((L, D, N_EXPERTS), D ** -0.5)
    moe_b_router = nrm((L, N_EXPERTS), 0.01)
    moe_w_gu = nrm((L, N_EXPERTS, D, 2 * EXPERT_FF), D ** -0.5)
    moe_b_gu = nrm((L, N_EXPERTS, 2 * EXPERT_FF), 0.02)
    moe_w_dn = nrm((L, N_EXPERTS, EXPERT_FF, D), EXPERT_FF ** -0.5)
    moe_b_dn = nrm((L, N_EXPERTS, D), 0.02)
    return {'x': x, 'c': c, 'ctx': ctx, 'c_ctx': c_ctx,
            'ada_dn': ada_dn, 'ada_up': ada_up, 'ada_b': ada_b, 'norm1': norm1, 'norm2': norm2,
            'w_in': w_in, 'gdn_conv': gdn_conv, 'gdn_a_log': gdn_a_log, 'gdn_dt_bias': gdn_dt_bias,
            'gdn_norm': gdn_norm, 'mla_q_norm': mla_q_norm, 'mla_kv_norm': mla_kv_norm,
            'mla_w_uq': mla_w_uq, 'mla_w_ukv': mla_w_ukv, 'mla_qn_norm': mla_qn_norm,
            'mla_qr_norm': mla_qr_norm, 'mla_kn_norm': mla_kn_norm, 'mla_kr_norm': mla_kr_norm,
            's5_lam_re': s5_lam_re, 's5_lam_im': s5_lam_im, 's5_log_step': s5_log_step,
            's5_b_re': s5_b_re, 's5_b_im': s5_b_im, 's5_c_re': s5_c_re, 's5_c_im': s5_c_im,
            's5_d': s5_d, 's5_w_glu': s5_w_glu, 's5_b_glu': s5_b_glu, 'w_out': w_out,
            'moe_w_router': moe_w_router, 'moe_b_router': moe_b_router, 'moe_w_gu': moe_w_gu,
            'moe_b_gu': moe_b_gu, 'moe_w_dn': moe_w_dn, 'moe_b_dn': moe_b_dn}


def reference(x, c, ctx, c_ctx, ada_dn, ada_up, ada_b, norm1, norm2, w_in, gdn_conv, gdn_a_log,
              gdn_dt_bias, gdn_norm, mla_q_norm, mla_kv_norm, mla_w_uq, mla_w_ukv, mla_qn_norm,
              mla_qr_norm, mla_kn_norm, mla_kr_norm, s5_lam_re, s5_lam_im, s5_log_step, s5_b_re,
              s5_b_im, s5_c_re, s5_c_im, s5_d, s5_w_glu, s5_b_glu, w_out, moe_w_router,
              moe_b_router, moe_w_gu, moe_b_gu, moe_w_dn, moe_b_dn):
    cos, sin = _axial_rope(x.shape[1])
    xc = ctx
    for layer in range(DEPTH):
        lp = {'ada_dn': ada_dn[layer], 'ada_up': ada_up[layer], 'ada_b': ada_b[layer],
              'norm1': norm1[layer], 'norm2': norm2[layer], 'w_in': w_in[layer],
              'gdn_conv': gdn_conv[layer], 'gdn_a_log': gdn_a_log[layer],
              'gdn_dt_bias': gdn_dt_bias[layer], 'gdn_norm': gdn_norm[layer],
              'mla_q_norm': mla_q_norm[layer], 'mla_kv_norm': mla_kv_norm[layer],
              'mla_w_uq': mla_w_uq[layer], 'mla_w_ukv': mla_w_ukv[layer],
              'mla_qn_norm': mla_qn_norm[layer], 'mla_qr_norm': mla_qr_norm[layer],
              'mla_kn_norm': mla_kn_norm[layer], 'mla_kr_norm': mla_kr_norm[layer],
              's5_lam_re': s5_lam_re[layer], 's5_lam_im': s5_lam_im[layer],
              's5_log_step': s5_log_step[layer], 's5_b_re': s5_b_re[layer], 's5_b_im': s5_b_im[layer],
              's5_c_re': s5_c_re[layer], 's5_c_im': s5_c_im[layer], 's5_d': s5_d[layer],
              's5_w_glu': s5_w_glu[layer], 's5_b_glu': s5_b_glu[layer], 'w_out': w_out[layer],
              'moe_w_router': moe_w_router[layer], 'moe_b_router': moe_b_router[layer],
              'moe_w_gu': moe_w_gu[layer], 'moe_b_gu': moe_b_gu[layer],
              'moe_w_dn': moe_w_dn[layer], 'moe_b_dn': moe_b_dn[layer]}
        x, xc = _layer(x, xc, c, c_ctx, lp, cos, sin, layer == DEPTH - 1)
    return x
```

```python
import functools
import math

import jax
import jax.numpy as jnp
from jax import lax
from jax.experimental import pallas as pl
from jax.experimental.pallas import tpu as pltpu

F32 = jnp.float32
BF16 = jnp.bfloat16
HIGHEST = lax.Precision.HIGHEST

D_MODEL = 4096
GRID_W = 64
GDN_HEADS = 12
GDN_DK = 128
GDN_DV = 128
GDN_CONV = 5
GDN_CHUNK = 64
MLA_HEADS = 12
MLA_Q_RANK = 1024
MLA_KV_RANK = 512
MLA_NOPE = 128
MLA_ROPE = 64
MLA_V = 128
MLA_SCALE = (MLA_NOPE + MLA_ROPE) ** -0.5
ROPE_THETA = 10000.0
S5_WIDTH = 1024
S5_GROUP = 16
S5_GROUPS = S5_WIDTH // S5_GROUP
S5_STATE = 64
GDN_QK = GDN_HEADS * GDN_DK
GDN_VW = GDN_HEADS * GDN_DV
GDN_NQKV = 2 * GDN_QK + GDN_VW
N_MOD = 6
N_EXPERTS = 32
TOP_K = 4
EXPERT_FF = 384
SWIGLU_LIMIT = 7.0
SWIGLU_ALPHA = 1.702
RMS_EPS = 1e-6

LANES = 128
SUBLANES = 8
VMEM_LIMIT_BYTES = 56 * 1024 * 1024

MLA_QP = 2 * LANES
COL_QKV = 0
COL_Z = GDN_NQKV
COL_MQ = COL_Z + GDN_VW
COL_S5 = COL_MQ + MLA_Q_RANK
COL_MKV = COL_S5 + S5_WIDTH
COL_BG = COL_MKV + MLA_KV_RANK
COL_KR = COL_BG + LANES
N_IN_PAD = COL_KR + LANES

S5_SLABS = S5_WIDTH // LANES
S5_SLAB_STATE = (S5_GROUPS // S5_SLABS) * S5_STATE


def _cparams(sem):
    return pltpu.CompilerParams(dimension_semantics=sem, vmem_limit_bytes=VMEM_LIMIT_BYTES)


def _tile(n, pref):
    if n <= pref:
        return n
    t = pref
    while n % t:
        t -= 1
    return t


def _silu(x):
    return x * jax.nn.sigmoid(x)


def _dot(a, b, dims=(((1,), (0,)), ((), ())), exact=False):
    if exact:
        return lax.dot_general(a.astype(F32), b.astype(F32), dims, precision=HIGHEST,
                               preferred_element_type=F32)
    return lax.dot_general(a.astype(BF16), b.astype(BF16), dims, preferred_element_type=F32)


_NT = (((1,), (1,)), ((), ()))
_TN = (((0,), (0,)), ((), ()))


def _mm_kernel(*refs, prologue, exact, has_bias, has_gain):
    it = iter(refs)
    a_ref = next(it)
    w_ref = next(it)
    b_ref = next(it) if has_bias else None
    g_ref = next(it) if has_gain else None
    o_ref = next(it)
    a = a_ref[...]
    if prologue == "silu":
        a = _silu(a.astype(F32))
    elif prologue == "rms":
        af = a.astype(F32)
        a = af * lax.rsqrt(jnp.mean(af * af, axis=-1, keepdims=True) + RMS_EPS) * g_ref[...]
    acc = _dot(a, w_ref[...], exact=exact)
    if has_bias:
        acc = acc + b_ref[...]
    o_ref[...] = acc.astype(o_ref.dtype)


def _mm(a, w, *, bias=None, gain=None, prologue=None, exact=False, out_dtype=F32,
        tm=1024, tn=1024, a_colblk=0, name="mm"):
    m = a.shape[0]
    k, n = w.shape
    tm = _tile(m, tm)
    tn = _tile(n, tn)
    in_specs = [pl.BlockSpec((tm, k), lambda i, j: (i, a_colblk)),
                pl.BlockSpec((k, tn), lambda i, j: (0, j))]
    args = [a, w]
    if bias is not None:
        in_specs.append(pl.BlockSpec((1, tn), lambda i, j: (0, j)))
        args.append(bias.reshape(1, n).astype(F32))
    if gain is not None:
        in_specs.append(pl.BlockSpec((1, k), lambda i, j: (0, 0)))
        args.append(gain.reshape(1, k).astype(F32))
    return pl.pallas_call(
        functools.partial(_mm_kernel, prologue=prologue, exact=exact,
                          has_bias=bias is not None, has_gain=gain is not None),
        out_shape=jax.ShapeDtypeStruct((m, n), out_dtype),
        grid=(m // tm, n // tn),
        in_specs=in_specs,
        out_specs=pl.BlockSpec((tm, tn), lambda i, j: (i, j)),
        compiler_params=_cparams(("parallel", "arbitrary")),
        name=name,
    )(*args)


def _norm_mod_kernel(x_ref, g_ref, sh_ref, sc_ref, o_ref):
    x = x_ref[...]
    y = x * lax.rsqrt(jnp.mean(x * x, axis=-1, keepdims=True) + RMS_EPS) * g_ref[...]
    o_ref[...] = (y * (1.0 + sc_ref[...]) + sh_ref[...]).astype(o_ref.dtype)


def _mod_spec(which, rows_per_mod, tm, tn=None):
    width = D_MODEL if tn is None else tn
    per_tile = rows_per_mod // tm
    if tn is None:
        return pl.BlockSpec((None, 1, width), lambda i: ((i // per_tile) * N_MOD + which, 0, 0))
    return pl.BlockSpec((None, 1, width), lambda i, j: ((i // per_tile) * N_MOD + which, 0, j))


def _norm_mod(x, gain, mods, which_shift, which_scale, rows_per_mod):
    m = x.shape[0]
    tm = _tile(math.gcd(m, rows_per_mod), 512)
    return pl.pallas_call(
        _norm_mod_kernel,
        out_shape=jax.ShapeDtypeStruct((m, D_MODEL), BF16),
        grid=(m // tm,),
        in_specs=[pl.BlockSpec((tm, D_MODEL), lambda i: (i, 0)),
                  pl.BlockSpec((1, D_MODEL), lambda i: (0, 0)),
                  _mod_spec(which_shift, rows_per_mod, tm),
                  _mod_spec(which_scale, rows_per_mod, tm)],
        out_specs=pl.BlockSpec((tm, D_MODEL), lambda i: (i, 0)),
        compiler_params=_cparams(("parallel",)),
        name="norm_mod",
    )(x, gain.reshape(1, D_MODEL), mods, mods)


def _norm_route_kernel(x_ref, g_ref, sh_ref, sc_ref, wr_ref, br_ref, h_ref, idx_ref, wt_ref):
    x = x_ref[...]
    y = x * lax.rsqrt(jnp.mean(x * x, axis=-1, keepdims=True) + RMS_EPS) * g_ref[...]
    h = y * (1.0 + sc_ref[...]) + sh_ref[...]
    h_ref[...] = h.astype(h_ref.dtype)
    logits = _dot(h, wr_ref[...], exact=True) + br_ref[...]
    lane = lax.broadcasted_iota(jnp.int32, logits.shape, 1).astype(F32)
    neg = jnp.float32(-jnp.inf)
    cur = jnp.where(lane < N_EXPERTS, logits, neg)
    idx_out = jnp.zeros(logits.shape, F32)
    val_out = jnp.full(logits.shape, neg, F32)
    for kk in range(TOP_K):
        mx = jnp.max(cur, axis=-1, keepdims=True)
        sel = jnp.min(jnp.where(cur == mx, lane, float(LANES)), axis=-1, keepdims=True)
        idx_out = jnp.where(lane == kk, sel, idx_out)
        val_out = jnp.where(lane == kk, mx, val_out)
        cur = jnp.where(lane == sel, neg, cur)
    top = jnp.max(val_out, axis=-1, keepdims=True)
    e = jnp.exp(val_out - top)
    wt_ref[...] = e / jnp.sum(e, axis=-1, keepdims=True)
    idx_ref[...] = idx_out.astype(jnp.int32)


def _norm_route(x, gain, mods, which_shift, which_scale, rows_per_mod, w_router, b_router):
    m = x.shape[0]
    tm = _tile(math.gcd(m, rows_per_mod), 512)
    row = lambda i: (i, 0)
    return pl.pallas_call(
        _norm_route_kernel,
        out_shape=(jax.ShapeDtypeStruct((m, D_MODEL), BF16),
                   jax.ShapeDtypeStruct((m, LANES), jnp.int32),
                   jax.ShapeDtypeStruct((m, LANES), F32)),
        grid=(m // tm,),
        in_specs=[pl.BlockSpec((tm, D_MODEL), row),
                  pl.BlockSpec((1, D_MODEL), lambda i: (0, 0)),
                  _mod_spec(which_shift, rows_per_mod, tm),
                  _mod_spec(which_scale, rows_per_mod, tm),
                  pl.BlockSpec((D_MODEL, LANES), lambda i: (0, 0)),
                  pl.BlockSpec((1, LANES), lambda i: (0, 0))],
        out_specs=(pl.BlockSpec((tm, D_MODEL), row),
                   pl.BlockSpec((tm, LANES), row),
                   pl.BlockSpec((tm, LANES), row)),
        compiler_params=_cparams(("parallel",)),
        name="norm_route",
    )(x, gain.reshape(1, D_MODEL), mods, mods, w_router, b_router)


def _gdn_prep_kernel(p_ref, w_ref, o_ref, *, n_qk_blocks, n_q_blocks, heads_per_block):
    j = pl.program_id(1)
    x = p_ref[...]
    n = x.shape[0]
    w = w_ref[...]
    t = lax.broadcasted_iota(jnp.int32, x.shape, 0)
    acc = x * w[GDN_CONV // 2:GDN_CONV // 2 + 1, :]
    for tap in range(GDN_CONV):
        off = tap - GDN_CONV // 2
        if off == 0:
            continue
        shifted = pltpu.roll(x, (-off) % n, 0)
        valid = (t + off >= 0) & (t + off < n)
        acc = acc + jnp.where(valid, shifted, 0.0) * w[tap:tap + 1, :]
    y = _silu(acc)

    @pl.when(j < n_qk_blocks)
    def _():
        scale = jnp.where(j < n_q_blocks, jnp.float32(GDN_DK ** -0.5), jnp.float32(1.0))
        for hh in range(heads_per_block):
            yh = y[:, hh * GDN_DK:(hh + 1) * GDN_DK]
            r = lax.rsqrt(jnp.sum(yh * yh, axis=-1, keepdims=True) + RMS_EPS)
            o_ref[:, hh * GDN_DK:(hh + 1) * GDN_DK] = yh * (r * scale)

    @pl.when(j >= n_qk_blocks)
    def _():
        o_ref[...] = y


def _gdn_prep(p, conv_w, batch, seg):
    hpb = 4
    wblk = hpb * GDN_DK
    return pl.pallas_call(
        functools.partial(_gdn_prep_kernel, n_qk_blocks=2 * GDN_QK // wblk,
                          n_q_blocks=GDN_QK // wblk, heads_per_block=hpb),
        out_shape=jax.ShapeDtypeStruct((batch * seg, GDN_NQKV), F32),
        grid=(batch, GDN_NQKV // wblk),
        in_specs=[pl.BlockSpec((seg, wblk), lambda b, j: (b, j)),
                  pl.BlockSpec((GDN_CONV, wblk), lambda b, j: (0, j))],
        out_specs=pl.BlockSpec((seg, wblk), lambda b, j: (b, j)),
        compiler_params=_cparams(("parallel", "arbitrary")),
        name="gdn_prep",
    )(p, conv_w)


def _unit_tri_inverse(a, eye, blk16, blk32):
    n1 = -jnp.where(blk16, a, 0.0)
    n2 = _dot(n1, n1, exact=True)
    n4 = _dot(n2, n2, exact=True)
    n8 = _dot(n4, n4, exact=True)
    p = _dot(eye + n1, eye + n2, exact=True)
    p = _dot(p, eye + n4, exact=True)
    p = _dot(p, eye + n8, exact=True)
    a1 = jnp.where(blk32 & jnp.logical_not(blk16), a, 0.0)
    p = p - _dot(p, _dot(a1, p, exact=True), exact=True)
    a2 = jnp.where(blk32, 0.0, a)
    p = p - _dot(p, _dot(a2, p, exact=True), exact=True)
    return p


def _gdn_kernel(q_ref, k_ref, v_ref, sc_ref, sct_ref, s0_ref, o_ref, sout_ref, s_scr,
                *, heads_per_block, n_chunks):
    hb = pl.program_id(1)
    d = pl.program_id(2)
    s = pl.program_id(3)
    cs = GDN_CHUNK

    @pl.when(s == 0)
    def _():
        s_scr[...] = s0_ref[...]

    sgn = 1 - 2 * d
    ii = lax.broadcasted_iota(jnp.int32, (cs, cs), 0)
    jj = lax.broadcasted_iota(jnp.int32, (cs, cs), 1)
    rel = (ii - jj) * sgn
    incl = rel >= 0
    strict = rel > 0
    eye = jnp.where(ii == jj, 1.0, 0.0).astype(F32)
    blk16 = (ii // 16) == (jj // 16)
    blk32 = (ii // 32) == (jj // 32)
    lane = lax.broadcasted_iota(jnp.int32, (cs, LANES), 1)
    neg = jnp.float32(-jnp.inf)

    def chunk(ci, carry):
        c = jnp.where(d == 0, ci, n_chunks - 1 - ci)
        r0 = pl.multiple_of(c * cs, cs)
        sc = sc_ref[pl.ds(r0, cs), :]
        for hh in range(heads_per_block):
            head = hb * heads_per_block + hh
            i_beta = d * GDN_HEADS + head
            i_g = 2 * GDN_HEADS + i_beta
            beta = jnp.sum(jnp.where(lane == i_beta, sc, 0.0), axis=1, keepdims=True)
            gc = jnp.sum(jnp.where(lane == i_g, sc, 0.0), axis=1, keepdims=True)
            gr = sct_ref[c, pl.ds(i_g, 1), :]
            g_last = jnp.where(d == 0, gr[:, cs - 1:cs], gr[:, 0:1])
            cols = slice(hh * GDN_DK, (hh + 1) * GDN_DK)
            q = q_ref[pl.ds(r0, cs), cols]
            k = k_ref[pl.ds(r0, cs), cols]
            v = v_ref[pl.ds(r0, cs), cols]
            st = s_scr[hh]

            decay = jnp.exp(jnp.where(incl, gc - gr, neg))
            kb = k * beta
            a = jnp.where(strict, _dot(kb, k, _NT) * decay, 0.0)
            tinv = _unit_tri_inverse(a, eye, blk16, blk32)
            rhs = jnp.concatenate([v * beta, kb * jnp.exp(gc)], axis=1)
            uw = _dot(tinv, rhs)
            v_new = uw[:, :GDN_DV] - _dot(uw[:, GDN_DV:], st)
            qk = _dot(q, k, _NT) * decay
            o = _dot(q * jnp.exp(gc), st) + _dot(qk, v_new)
            st_new = st * jnp.exp(g_last) + _dot(k * jnp.exp(g_last - gc), v_new, _TN)
            o_ref[pl.ds(r0, cs), cols] = o
            s_scr[hh] = st_new
        return carry

    lax.fori_loop(0, n_chunks, chunk, 0)

    @pl.when(s == pl.num_programs(3) - 1)
    def _():
        sout_ref[...] = s_scr[...]


def _gdn_scan(qkv, scal, scal_t, s0, batch, seg):
    hpb = 4
    wblk = hpb * GDN_DK
    tb = _tile(seg, 256)
    nblk = seg // tb
    n_hb = GDN_HEADS // hpb
    k_off = GDN_QK // wblk
    v_off = 2 * GDN_QK // wblk

    def rowblk(b, d, s):
        return b * nblk + jnp.where(d == 0, s, nblk - 1 - s)

    state_spec = pl.BlockSpec((None, None, hpb, GDN_DK, GDN_DV), lambda b, h, d, s: (d, b, h, 0, 0))
    return pl.pallas_call(
        functools.partial(_gdn_kernel, heads_per_block=hpb, n_chunks=tb // GDN_CHUNK),
        out_shape=(jax.ShapeDtypeStruct((2, batch * seg, GDN_VW), F32),
                   jax.ShapeDtypeStruct(s0.shape, F32)),
        grid=(batch, n_hb, 2, nblk),
        in_specs=[pl.BlockSpec((tb, wblk), lambda b, h, d, s: (rowblk(b, d, s), h)),
                  pl.BlockSpec((tb, wblk), lambda b, h, d, s: (rowblk(b, d, s), k_off + h)),
                  pl.BlockSpec((tb, wblk), lambda b, h, d, s: (rowblk(b, d, s), v_off + h)),
                  pl.BlockSpec((tb, LANES), lambda b, h, d, s: (rowblk(b, d, s), 0)),
                  pl.BlockSpec((tb // GDN_CHUNK, LANES, GDN_CHUNK),
                               lambda b, h, d, s: (rowblk(b, d, s), 0, 0)),
                  state_spec],
        out_specs=(pl.BlockSpec((None, tb, wblk), lambda b, h, d, s: (d, rowblk(b, d, s), h)),
                   state_spec),
        scratch_shapes=[pltpu.VMEM((hpb, GDN_DK, GDN_DV), F32)],
        compiler_params=_cparams(("parallel", "parallel", "parallel", "arbitrary")),
        name="gdn_scan",
    )(qkv, qkv, qkv, scal, scal_t, s0)


def _gdn_scalars(p, a_log, dt_bias):
    r = p.shape[0]
    nh = GDN_HEADS
    bg = p[:, COL_BG:COL_BG + 4 * nh]
    beta = jax.nn.sigmoid(bg[:, :2 * nh])
    g = -jnp.exp(a_log.astype(F32)).reshape(1, 2 * nh) * jax.nn.softplus(
        bg[:, 2 * nh:] + dt_bias.astype(F32).reshape(1, 2 * nh))
    gch = g.reshape(r // GDN_CHUNK, GDN_CHUNK, 2 * nh)
    gc_f = jnp.cumsum(gch[..., :nh], axis=1)
    gc_b = jnp.flip(jnp.cumsum(jnp.flip(gch[..., nh:], axis=1), axis=1), axis=1)
    gc = jnp.concatenate([gc_f, gc_b], axis=-1).reshape(r, 2 * nh)
    scal = jnp.concatenate([beta, gc, jnp.zeros((r, LANES - 4 * nh), F32)], axis=1)
    scal_t = jnp.transpose(scal.reshape(r // GDN_CHUNK, GDN_CHUNK, LANES), (0, 2, 1))
    return scal, scal_t


def _gdn_out_kernel(o_ref, z_ref, g_ref, y_ref):
    o = o_ref[0] + o_ref[1]
    z = z_ref[...]
    gain = g_ref[...]
    for h in range(GDN_HEADS):
        cols = slice(h * GDN_DV, (h + 1) * GDN_DV)
        oh = o[:, cols]
        yh = oh * lax.rsqrt(jnp.mean(oh * oh, axis=-1, keepdims=True) + RMS_EPS) * gain
        y_ref[:, cols] = (yh * _silu(z[:, cols])).astype(y_ref.dtype)


def _gdn_out(o2, p, gain):
    m = p.shape[0]
    tm = _tile(m, 256)
    return pl.pallas_call(
        _gdn_out_kernel,
        out_shape=jax.ShapeDtypeStruct((m, GDN_VW), BF16),
        grid=(m // tm,),
        in_specs=[pl.BlockSpec((2, tm, GDN_VW), lambda i: (0, i, 0)),
                  pl.BlockSpec((tm, GDN_VW), lambda i: (i, COL_Z // GDN_VW)),
                  pl.BlockSpec((1, GDN_DV), lambda i: (0, 0))],
        out_specs=pl.BlockSpec((tm, GDN_VW), lambda i: (i, 0)),
        compiler_params=_cparams(("parallel",)),
        name="gdn_out",
    )(o2, p, gain.reshape(1, GDN_DV))


def _mla_prep_kernel(q_ref, kv_ref, kr_ref, cos_ref, sin_ref, gq_ref, gqr_ref, gk_ref, gkr_ref,
                     qo_ref, ko_ref, vo_ref, *, rope):
    lane = lax.broadcasted_iota(jnp.int32, (q_ref.shape[0], LANES), 1)

    def rms(x, g, width):
        return x * lax.rsqrt(jnp.sum(x * x, axis=-1, keepdims=True) * (1.0 / width) + RMS_EPS) * g

    def rot(x):
        if not rope:
            return x
        half = MLA_ROPE // 2
        swapped = jnp.where(lane < half, pltpu.roll(x, LANES - half, 1), pltpu.roll(x, half, 1))
        return x * cos_ref[...] + swapped * sin_ref[...]

    q = q_ref[...]
    kv = kv_ref[...]
    qn = rms(q[:, :MLA_NOPE], gq_ref[...], MLA_NOPE)
    qr = rot(rms(q[:, MLA_NOPE:], gqr_ref[...], MLA_ROPE))
    qo_ref[:, :MLA_NOPE] = (qn * MLA_SCALE).astype(qo_ref.dtype)
    qo_ref[:, MLA_NOPE:] = (qr * MLA_SCALE).astype(qo_ref.dtype)
    kn = rms(kv[:, :MLA_NOPE], gk_ref[...], MLA_NOPE)
    kr = rot(rms(kr_ref[...], gkr_ref[...], MLA_ROPE))
    ko_ref[:, :MLA_NOPE] = kn.astype(ko_ref.dtype)
    ko_ref[:, MLA_NOPE:] = kr.astype(ko_ref.dtype)
    vo_ref[...] = kv[:, MLA_NOPE:].astype(vo_ref.dtype)


def _mla_prep(q, kv, p, cos_t, sin_t, gains, seg, rope):
    m = q.shape[0]
    tm = _tile(seg, 512)
    tpb = seg // tm
    gq, gqr, gk, gkr = gains
    vec = pl.BlockSpec((1, LANES), lambda i, h: (0, 0))
    tab = pl.BlockSpec((tm, LANES), lambda i, h: (i % tpb, 0))
    return pl.pallas_call(
        functools.partial(_mla_prep_kernel, rope=rope),
        out_shape=(jax.ShapeDtypeStruct((m, MLA_HEADS * MLA_QP), BF16),
                   jax.ShapeDtypeStruct((m, MLA_HEADS * MLA_QP), BF16),
                   jax.ShapeDtypeStruct((m, MLA_HEADS * MLA_V), BF16)),
        grid=(m // tm, MLA_HEADS),
        in_specs=[pl.BlockSpec((tm, MLA_QP), lambda i, h: (i, h)),
                  pl.BlockSpec((tm, MLA_NOPE + MLA_V), lambda i, h: (i, h)),
                  pl.BlockSpec((tm, LANES), lambda i, h: (i, COL_KR // LANES)),
                  tab, tab, vec, vec, vec, vec],
        out_specs=(pl.BlockSpec((tm, MLA_QP), lambda i, h: (i, h)),
                   pl.BlockSpec((tm, MLA_QP), lambda i, h: (i, h)),
                   pl.BlockSpec((tm, MLA_V), lambda i, h: (i, h))),
        compiler_params=_cparams(("parallel", "arbitrary")),
        name="mla_prep",
    )(q, kv, p, cos_t, sin_t, gq, gqr, gk, gkr)


def _attn_kernel(*refs, with_latent):
    if with_latent:
        q_ref, kc_ref, vc_ref, kl_ref, vl_ref, o_ref = refs
    else:
        q_ref, kc_ref, vc_ref, o_ref = refs
    q = q_ref[...]
    s_c = _dot(q, kc_ref[...], _NT)
    m = jnp.max(s_c, axis=-1, keepdims=True)
    if with_latent:
        s_l = _dot(q, kl_ref[...], _NT)
        m = jnp.maximum(m, jnp.max(s_l, axis=-1, keepdims=True))
    p_c = jnp.exp(s_c - m)
    den = jnp.sum(p_c, axis=-1, keepdims=True)
    acc = _dot(p_c, vc_ref[...])
    if with_latent:
        p_l = jnp.exp(s_l - m)
        den = den + jnp.sum(p_l, axis=-1, keepdims=True)
        acc = acc + _dot(p_l, vl_ref[...])
    o_ref[...] = (acc / den).astype(o_ref.dtype)


def _attention(qh, kh_c, vh_c, kh_l, vh_l, batch, n_q, n_c, n_l):
    with_latent = kh_l is not None
    tq = _tile(n_q, 512)
    nqt = n_q // tq
    in_specs = [pl.BlockSpec((tq, MLA_QP), lambda b, h, i: (b * nqt + i, h)),
                pl.BlockSpec((n_c, MLA_QP), lambda b, h, i: (b, h)),
                pl.BlockSpec((n_c, MLA_V), lambda b, h, i: (b, h))]
    args = [qh, kh_c, vh_c]
    if with_latent:
        in_specs += [pl.BlockSpec((n_l, MLA_QP), lambda b, h, i: (b, h)),
                     pl.BlockSpec((n_l, MLA_V), lambda b, h, i: (b, h))]
        args += [kh_l, vh_l]
    return pl.pallas_call(
        functools.partial(_attn_kernel, with_latent=with_latent),
        out_shape=jax.ShapeDtypeStruct((batch * n_q, MLA_HEADS * MLA_V), BF16),
        grid=(batch, MLA_HEADS, nqt),
        in_specs=in_specs,
        out_specs=pl.BlockSpec((tq, MLA_V), lambda b, h, i: (b * nqt + i, h)),
        compiler_params=_cparams(("parallel", "parallel", "arbitrary")),
        name="mla_attn",
    )(*args)


def _s5_kernel(u_ref, w_ref, a_ref, c_ref, h0_ref, y_ref, hout_ref, buf, h_scr, *, steps, batch):
    d = pl.program_id(0)
    s = pl.program_id(2)
    ns = S5_SLAB_STATE

    @pl.when(s == 0)
    def _():
        h_scr[...] = h0_ref[...]

    u2 = u_ref[...].reshape(steps * batch, LANES)
    buf[...] = _dot(u2, w_ref[...])
    a = a_ref[...]
    ar = jnp.broadcast_to(a[0:1, :], (batch, ns))
    ai = jnp.broadcast_to(a[1:2, :], (batch, ns))

    def step(i, carry):
        hr, hi = carry
        t = jnp.where(d == 0, i, steps - 1 - i)
        r0 = pl.multiple_of(t * batch, batch)
        nr = ar * hr - ai * hi + buf[pl.ds(r0, batch), :ns]
        ni = ar * hi + ai * hr + buf[pl.ds(r0, batch), ns:]
        buf[pl.ds(r0, batch), :ns] = nr
        buf[pl.ds(r0, batch), ns:] = ni
        return nr, ni

    hr, hi = lax.fori_loop(0, steps, step, (h_scr[:, :ns], h_scr[:, ns:]))
    h_scr[:, :ns] = hr
    h_scr[:, ns:] = hi
    y_ref[...] = _dot(buf[...], c_ref[...]).reshape(steps, batch, LANES)

    @pl.when(s == pl.num_programs(2) - 1)
    def _():
        hout_ref[...] = h_scr[...]


def _s5_scan(u_t, w_bu, a_pack, c_pack, h0):
    seg, batch, _ = u_t.shape
    steps = _tile(seg, 128)
    nblk = seg // steps

    def tblk(d, s):
        return jnp.where(d == 0, s, nblk - 1 - s)

    st_spec = pl.BlockSpec((None, None, batch, 2 * S5_SLAB_STATE), lambda d, j, s: (d, j, 0, 0))
    return pl.pallas_call(
        functools.partial(_s5_kernel, steps=steps, batch=batch),
        out_shape=(jax.ShapeDtypeStruct((2, seg, batch, S5_WIDTH), F32),
                   jax.ShapeDtypeStruct(h0.shape, F32)),
        grid=(2, S5_SLABS, nblk),
        in_specs=[pl.BlockSpec((steps, batch, LANES), lambda d, j, s: (tblk(d, s), 0, j)),
                  pl.BlockSpec((None, None, LANES, 2 * S5_SLAB_STATE), lambda d, j, s: (d, j, 0, 0)),
                  pl.BlockSpec((None, None, 2, S5_SLAB_STATE), lambda d, j, s: (d, j, 0, 0)),
                  pl.BlockSpec((None, None, 2 * S5_SLAB_STATE, LANES), lambda d, j, s: (d, j, 0, 0)),
                  st_spec],
        out_specs=(pl.BlockSpec((None, steps, batch, LANES), lambda d, j, s: (d, tblk(d, s), 0, j)),
                   st_spec),
        scratch_shapes=[pltpu.VMEM((steps * batch, 2 * S5_SLAB_STATE), F32),
                        pltpu.VMEM((batch, 2 * S5_SLAB_STATE), F32)],
        compiler_params=_cparams(("parallel", "parallel", "arbitrary")),
        name="s5_scan",
    )(u_t, w_bu, a_pack, c_pack, h0)


def _s5_glu_kernel(y_ref, u_ref, d_ref, w_ref, b_ref, o_ref):
    y = y_ref[0] + y_ref[1] + d_ref[...] * u_ref[...]
    z = _dot(jax.nn.gelu(y), w_ref[...]) + b_ref[...]
    o_ref[...] = (z[:, :S5_WIDTH] * jax.nn.sigmoid(z[:, S5_WIDTH:])).astype(o_ref.dtype)


def _s5_glu(y2, p, d_skip, w_glu, b_glu, batch, seg):
    tl = _tile(seg, 512)
    nt = seg // tl
    return pl.pallas_call(
        _s5_glu_kernel,
        out_shape=jax.ShapeDtypeStruct((batch * seg, S5_WIDTH), BF16),
        grid=(batch, nt),
        in_specs=[pl.BlockSpec((2, tl, S5_WIDTH), lambda b, i: (0, i, b)),
                  pl.BlockSpec((tl, S5_WIDTH), lambda b, i: (b * nt + i, COL_S5 // S5_WIDTH)),
                  pl.BlockSpec((1, S5_WIDTH), lambda b, i: (0, 0)),
                  pl.BlockSpec((S5_WIDTH, 2 * S5_WIDTH), lambda b, i: (0, 0)),
                  pl.BlockSpec((1, 2 * S5_WIDTH), lambda b, i: (0, 0))],
        out_specs=pl.BlockSpec((tl, S5_WIDTH), lambda b, i: (b * nt + i, 0)),
        compiler_params=_cparams(("parallel", "arbitrary")),
        name="s5_glu",
    )(y2, p, d_skip.reshape(1, S5_WIDTH), w_glu, b_glu.reshape(1, 2 * S5_WIDTH))


def _s5_params(lp):
    gps = S5_GROUPS // S5_SLABS
    eye = jnp.eye(gps, dtype=F32)
    w_l, a_l, c_l = [], [], []
    for d in range(2):
        lam_re = lp["s5_lam_re"][d].astype(F32)
        lam_im = lp["s5_lam_im"][d].astype(F32)
        dt = jnp.exp(lp["s5_log_step"][d].astype(F32))[:, None]
        mag = jnp.exp(lam_re * dt)
        ang = lam_im * dt
        a_re, a_im = mag * jnp.cos(ang), mag * jnp.sin(ang)
        den = lam_re * lam_re + lam_im * lam_im
        nr, ni = a_re - 1.0, a_im
        coef_re = ((nr * lam_re + ni * lam_im) / den)[..., None]
        coef_im = ((ni * lam_re - nr * lam_im) / den)[..., None]
        b_re, b_im = lp["s5_b_re"][d].astype(F32), lp["s5_b_im"][d].astype(F32)
        bb_re = coef_re * b_re - coef_im * b_im
        bb_im = coef_re * b_im + coef_im * b_re

        def bdiag_in(bb):
            x = bb.reshape(S5_SLABS, gps, S5_STATE, S5_GROUP)
            return jnp.einsum("jgph,gk->jghkp", x, eye).reshape(S5_SLABS, LANES, S5_SLAB_STATE)

        def bdiag_out(cc):
            x = cc.astype(F32).reshape(S5_SLABS, gps, S5_GROUP, S5_STATE)
            return jnp.einsum("jghp,gk->jgpkh", x, eye).reshape(S5_SLABS, S5_SLAB_STATE, LANES)

        w_l.append(jnp.concatenate([bdiag_in(bb_re), bdiag_in(bb_im)], axis=-1))
        a_l.append(jnp.stack([a_re.reshape(S5_SLABS, S5_SLAB_STATE),
                              a_im.reshape(S5_SLABS, S5_SLAB_STATE)], axis=1))
        c_l.append(jnp.concatenate([bdiag_out(lp["s5_c_re"][d]), -bdiag_out(lp["s5_c_im"][d])], axis=1))
    return jnp.stack(w_l).astype(BF16), jnp.stack(a_l), jnp.stack(c_l).astype(BF16)


def _wout_kernel(a1_ref, a2_ref, a3_ref, w1_ref, w2_ref, w3_ref, x_ref, g_ref, o_ref):
    acc = _dot(a1_ref[...], w1_ref[...])
    acc = acc + _dot(a2_ref[...], w2_ref[...])
    acc = acc + _dot(a3_ref[...], w3_ref[...])
    o_ref[...] = x_ref[...] + g_ref[...] * acc


def _wout(a1, a2, a3, w1, w2, w3, x, mods, which_gate, rows_per_mod):
    m = x.shape[0]
    tm = _tile(math.gcd(m, rows_per_mod), 1024)
    tn = 1024
    row = lambda i, j: (i, 0)
    col = lambda i, j: (0, j)
    return pl.pallas_call(
        _wout_kernel,
        out_shape=jax.ShapeDtypeStruct((m, D_MODEL), F32),
        grid=(m // tm, D_MODEL // tn),
        in_specs=[pl.BlockSpec((tm, a1.shape[1]), row),
                  pl.BlockSpec((tm, a2.shape[1]), row),
                  pl.BlockSpec((tm, a3.shape[1]), row),
                  pl.BlockSpec((w1.shape[0], tn), col),
                  pl.BlockSpec((w2.shape[0], tn), col),
                  pl.BlockSpec((w3.shape[0], tn), col),
                  pl.BlockSpec((tm, tn), lambda i, j: (i, j)),
                  _mod_spec(which_gate, rows_per_mod, tm, tn)],
        out_specs=pl.BlockSpec((tm, tn), lambda i, j: (i, j)),
        compiler_params=_cparams(("parallel", "arbitrary")),
        name="w_out",
    )(a1, a2, a3, w1, w2, w3, x, mods)


def _moe_kernel(te_ref, tv_ref, x_ref, wg_ref, wl_ref, bg_ref, bl_ref, wd_ref, bd_ref, gate_ref, y_ref):
    i = pl.program_id(0)

    @pl.when(tv_ref[i] > 0)
    def _():
        x = x_ref[...]
        x_glu = jnp.minimum(_dot(x, wg_ref[...]) + bg_ref[...], SWIGLU_LIMIT)
        x_lin = jnp.clip(_dot(x, wl_ref[...]) + bl_ref[...], -SWIGLU_LIMIT, SWIGLU_LIMIT)
        act = (x_lin + 1.0) * x_glu * jax.nn.sigmoid(SWIGLU_ALPHA * x_glu)
        y = _dot(act, wd_ref[...]) + bd_ref[...]
        y_ref[...] = (gate_ref[:, 0:1] * y).astype(y_ref.dtype)

    @pl.when(tv_ref[i] == 0)
    def _():
        y_ref[...] = jnp.zeros(y_ref.shape, y_ref.dtype)


def _moe_experts(xs, gate_rows, tile_expert, tile_valid, w_g, w_l, b_g, b_l, w_dn, b_dn, tm):
    rows = xs.shape[0]
    ff = w_g.shape[-1]
    grid_spec = pltpu.PrefetchScalarGridSpec(
        num_scalar_prefetch=2,
        grid=(rows // tm,),
        in_specs=[pl.BlockSpec((tm, D_MODEL), lambda i, te, tv: (i, 0)),
                  pl.BlockSpec((None, D_MODEL, ff), lambda i, te, tv: (te[i], 0, 0)),
                  pl.BlockSpec((None, D_MODEL, ff), lambda i, te, tv: (te[i], 0, 0)),
                  pl.BlockSpec((None, 1, ff), lambda i, te, tv: (te[i], 0, 0)),
                  pl.BlockSpec((None, 1, ff), lambda i, te, tv: (te[i], 0, 0)),
                  pl.BlockSpec((None, ff, D_MODEL), lambda i, te, tv: (te[i], 0, 0)),
                  pl.BlockSpec((None, 1, D_MODEL), lambda i, te, tv: (te[i], 0, 0)),
                  pl.BlockSpec((tm, LANES), lambda i, te, tv: (i, 0))],
        out_specs=pl.BlockSpec((tm, D_MODEL), lambda i, te, tv: (i, 0)),
    )
    return pl.pallas_call(
        _moe_kernel,
        out_shape=jax.ShapeDtypeStruct((rows, D_MODEL), BF16),
        grid_spec=grid_spec,
        compiler_params=_cparams(("arbitrary",)),
        name="moe_experts",
    )(tile_expert, tile_valid, xs, w_g, w_l, b_g, b_l, w_dn, b_dn, gate_rows)


def _moe_combine_kernel(y_ref, x_ref, g_ref, o_ref):
    acc = y_ref[0].astype(F32)
    for kk in range(1, TOP_K):
        acc = acc + y_ref[kk].astype(F32)
    o_ref[...] = x_ref[...] + g_ref[...] * acc


def _moe_combine(y4, x, mods, which_gate, rows_per_mod, row_off):
    m = x.shape[0]
    tm = _tile(math.gcd(math.gcd(m, rows_per_mod), row_off), 256)
    off = row_off // tm
    return pl.pallas_call(
        _moe_combine_kernel,
        out_shape=jax.ShapeDtypeStruct((m, D_MODEL), F32),
        grid=(m // tm,),
        in_specs=[pl.BlockSpec((TOP_K, tm, D_MODEL), lambda i: (0, off + i, 0)),
                  pl.BlockSpec((tm, D_MODEL), lambda i: (i, 0)),
                  _mod_spec(which_gate, rows_per_mod, tm)],
        out_specs=pl.BlockSpec((tm, D_MODEL), lambda i: (i, 0)),
        compiler_params=_cparams(("parallel",)),
        name="moe_combine",
    )(y4, x, mods)


def _moe_dispatch(idx, wts, tm):
    t = idx.shape[0]
    n = t * TOP_K
    e_flat = idx[:, :TOP_K].reshape(n)
    w_flat = wts[:, :TOP_K].reshape(n)
    order = jnp.argsort(e_flat, stable=True)
    e_sorted = e_flat[order]
    counts = jnp.zeros((N_EXPERTS,), jnp.int32).at[e_flat].add(1)
    padded = ((counts + tm - 1) // tm) * tm
    start = jnp.cumsum(counts) - counts
    start_p = jnp.cumsum(padded) - padded
    rank = jnp.arange(n, dtype=jnp.int32) - start[e_sorted]
    dest = start_p[e_sorted] + rank
    n_rows = ((n + N_EXPERTS * (tm - 1)) // tm + 1) * tm
    row_token = jnp.zeros((n_rows,), jnp.int32).at[dest].set((order // TOP_K).astype(jnp.int32))
    row_gate = jnp.zeros((n_rows,), F32).at[dest].set(w_flat[order])
    pos = jnp.zeros((n,), jnp.int32).at[order].set(dest).reshape(t, TOP_K)
    tile_start = jnp.arange(n_rows // tm, dtype=jnp.int32) * tm
    ends = jnp.cumsum(padded)
    tile_expert = jnp.minimum(jnp.searchsorted(ends, tile_start, side="right"), N_EXPERTS - 1).astype(jnp.int32)
    tile_valid = (tile_start < ends[-1]).astype(jnp.int32)
    return row_token, row_gate, pos, tile_expert, tile_valid


def _moe(h2_list, idx_list, wt_list, lw):
    tm = 256
    h2 = jnp.concatenate(h2_list, axis=0) if len(h2_list) > 1 else h2_list[0]
    idx = jnp.concatenate(idx_list, axis=0) if len(idx_list) > 1 else idx_list[0]
    wts = jnp.concatenate(wt_list, axis=0) if len(wt_list) > 1 else wt_list[0]
    row_token, row_gate, pos, tile_expert, tile_valid = _moe_dispatch(idx, wts, tm)
    xs = jnp.take(h2, row_token, axis=0)
    gate_rows = jnp.broadcast_to(row_gate[:, None], (row_gate.shape[0], LANES))
    ys = _moe_experts(xs, gate_rows, tile_expert, tile_valid, lw["moe_w_g"], lw["moe_w_l"],
                      lw["moe_b_g"], lw["moe_b_l"], lw["moe_w_dn"], lw["moe_b_dn"], tm)
    return jnp.take(ys, pos.T, axis=0)


def _prep_layer(lp):
    w = {}
    w_in = lp["w_in"]
    e_gdn = GDN_NQKV + GDN_VW + 4 * GDN_HEADS
    e_mla = e_gdn + MLA_Q_RANK + MLA_KV_RANK + MLA_ROPE
    o_bg = GDN_NQKV + GDN_VW
    zeros = lambda n: jnp.zeros((D_MODEL, n), w_in.dtype)
    w["w_in"] = jnp.concatenate([
        w_in[:, :o_bg],
        w_in[:, e_gdn:e_gdn + MLA_Q_RANK],
        w_in[:, e_mla:],
        w_in[:, e_gdn + MLA_Q_RANK:e_gdn + MLA_Q_RANK + MLA_KV_RANK],
        w_in[:, o_bg:e_gdn], zeros(LANES - 4 * GDN_HEADS),
        w_in[:, e_mla - MLA_ROPE:e_mla], zeros(LANES - MLA_ROPE),
    ], axis=1).astype(BF16)
    uq = lp["mla_w_uq"].reshape(MLA_Q_RANK, MLA_HEADS, MLA_NOPE + MLA_ROPE)
    uq = jnp.pad(uq, ((0, 0), (0, 0), (0, MLA_QP - MLA_NOPE - MLA_ROPE)))
    w["w_uq"] = uq.reshape(MLA_Q_RANK, MLA_HEADS * MLA_QP).astype(BF16)
    w["w_ukv"] = lp["mla_w_ukv"].astype(BF16)
    pad_r = lambda g: jnp.pad(g.astype(F32), (0, LANES - MLA_ROPE)).reshape(1, LANES)
    w["mla_gains"] = (lp["mla_qn_norm"].astype(F32).reshape(1, LANES), pad_r(lp["mla_qr_norm"]),
                      lp["mla_kn_norm"].astype(F32).reshape(1, LANES), pad_r(lp["mla_kr_norm"]))
    w["s5_w"], w["s5_a"], w["s5_c"] = _s5_params(lp)
    w["s5_w_glu"] = lp["s5_w_glu"].astype(BF16)
    wo = lp["w_out"].astype(BF16)
    w["w_out"] = (wo[:GDN_VW], wo[GDN_VW:GDN_VW + MLA_HEADS * MLA_V], wo[GDN_VW + MLA_HEADS * MLA_V:])
    w["w_router"] = jnp.pad(lp["moe_w_router"].astype(F32), ((0, 0), (0, LANES - N_EXPERTS)))
    w["b_router"] = jnp.pad(lp["moe_b_router"].astype(F32), (0, LANES - N_EXPERTS)).reshape(1, LANES)
    w["moe_w_g"] = lp["moe_w_gu"][:, :, 0::2].astype(BF16)
    w["moe_w_l"] = lp["moe_w_gu"][:, :, 1::2].astype(BF16)
    w["moe_b_g"] = lp["moe_b_gu"][:, None, 0::2].astype(F32)
    w["moe_b_l"] = lp["moe_b_gu"][:, None, 1::2].astype(F32)
    w["moe_w_dn"] = lp["moe_w_dn"].astype(BF16)
    w["moe_b_dn"] = lp["moe_b_dn"][:, None, :].astype(F32)
    return w


def _rope_tables(n_tokens):
    n_rows = n_tokens // GRID_W
    row = jnp.repeat(jnp.arange(n_rows), GRID_W).astype(F32)
    col = jnp.tile(jnp.arange(GRID_W), n_rows).astype(F32)
    axis_dim = MLA_ROPE // 2
    inv = ROPE_THETA ** (-jnp.arange(0, axis_dim, 2, dtype=F32) / axis_dim)
    ang = jnp.concatenate([row[:, None] * inv, col[:, None] * inv], axis=-1)
    cos, sin = jnp.cos(ang), jnp.sin(ang)
    pad = jnp.zeros((n_tokens, LANES - MLA_ROPE), F32)
    return (jnp.concatenate([cos, cos, pad], axis=1), jnp.concatenate([-sin, sin, pad], axis=1))


def _mixers(streams, lp, lw, rope_tabs, batch, ctx_out):
    (p_c, n_c), (p_l, n_l) = streams

    gdn_state = jnp.zeros((2, batch, GDN_HEADS, GDN_DK, GDN_DV), F32)
    gdn = []
    for p, seg in streams:
        qkv = _gdn_prep(p, lp["gdn_conv"].astype(F32), batch, seg)
        scal, scal_t = _gdn_scalars(p, lp["gdn_a_log"], lp["gdn_dt_bias"])
        o2, gdn_state = _gdn_scan(qkv, scal, scal_t, gdn_state, batch, seg)
        gdn.append(_gdn_out(o2, p, lp["gdn_norm"].astype(F32)) if (ctx_out or p is p_l) else None)

    heads = []
    for (p, seg), rope in zip(streams, (False, True)):
        q = _mm(p, lw["w_uq"], gain=lp["mla_q_norm"], prologue="rms", a_colblk=COL_MQ // MLA_Q_RANK,
                tn=1536, name="mla_uq")
        kv = _mm(p, lw["w_ukv"], gain=lp["mla_kv_norm"], prologue="rms", a_colblk=COL_MKV // MLA_KV_RANK,
                 tn=1536, name="mla_ukv")
        heads.append(_mla_prep(q, kv, p, rope_tabs[0], rope_tabs[1], lw["mla_gains"], seg, rope))
    (qh_c, kh_c, vh_c), (qh_l, kh_l, vh_l) = heads
    mla_l = _attention(qh_l, kh_c, vh_c, kh_l, vh_l, batch, n_l, n_c, n_l)
    mla_c = _attention(qh_c, kh_c, vh_c, None, None, batch, n_c, n_c, 0) if ctx_out else None

    s5_state = jnp.zeros((2, S5_SLABS, batch, 2 * S5_SLAB_STATE), F32)
    s5 = []
    for p, seg in streams:
        u_t = jnp.transpose(p[:, COL_S5:COL_S5 + S5_WIDTH].reshape(batch, seg, S5_WIDTH), (1, 0, 2))
        y2, s5_state = _s5_scan(u_t, lw["s5_w"], lw["s5_a"], lw["s5_c"], s5_state)
        if ctx_out or p is p_l:
            s5.append(_s5_glu(y2.reshape(2, seg, batch * S5_WIDTH), p, lp["s5_d"].astype(F32),
                              lw["s5_w_glu"], lp["s5_b_glu"].astype(F32), batch, seg))
        else:
            s5.append(None)
    return (gdn[0], mla_c, s5[0]), (gdn[1], mla_l, s5[1])


def _layer(x, xc, mods_l, mods_c, lp, rope_tabs, batch, n_l, n_c, last):
    lw = _prep_layer(lp)
    ctx_out = not last
    rows_c = batch * n_c
    h_l = _norm_mod(x, lp["norm1"], mods_l, 0, 1, n_l)
    h_c = _norm_mod(xc, lp["norm1"], mods_c, 0, 1, rows_c)
    p_l = _mm(h_l, lw["w_in"], tn=1280, name="w_in")
    p_c = _mm(h_c, lw["w_in"], tn=1280, name="w_in")
    mix_c, mix_l = _mixers([(p_c, n_c), (p_l, n_l)], lp, lw, rope_tabs, batch, ctx_out)
    x = _wout(*mix_l, *lw["w_out"], x, mods_l, 2, n_l)
    h2_l, idx_l, wt_l = _norm_route(x, lp["norm2"], mods_l, 3, 4, n_l, lw["w_router"], lw["b_router"])
    if last:
        y4 = _moe([h2_l], [idx_l], [wt_l], lw)
        return _moe_combine(y4, x, mods_l, 5, n_l, 0), xc
    xc = _wout(*mix_c, *lw["w_out"], xc, mods_c, 2, rows_c)
    h2_c, idx_c, wt_c = _norm_route(xc, lp["norm2"], mods_c, 3, 4, rows_c, lw["w_router"], lw["b_router"])
    y4 = _moe([h2_c, h2_l], [idx_c, idx_l], [wt_c, wt_l], lw)
    x_new = _moe_combine(y4, x, mods_l, 5, n_l, rows_c)
    xc_new = _moe_combine(y4, xc, mods_c, 5, rows_c, 0)
    return x_new, xc_new


def _ada_mods(cvecs, lp):
    r = cvecs.shape[0]
    rp = -(-r // SUBLANES) * SUBLANES
    cv = jnp.pad(cvecs.astype(F32), ((0, rp - r), (0, 0)))
    t = _mm(cv, lp["ada_dn"].astype(F32), prologue="silu", exact=True, name="ada_dn")
    m = _mm(t, lp["ada_up"].astype(F32), bias=lp["ada_b"], exact=True, tn=2048, name="ada_up")
    return m[:r].reshape(r * N_MOD, 1, D_MODEL)


def kernel(x, c, ctx, c_ctx, ada_dn, ada_up, ada_b, norm1, norm2, w_in, gdn_conv, gdn_a_log, gdn_dt_bias, gdn_norm, mla_q_norm, mla_kv_norm, mla_w_uq, mla_w_ukv, mla_qn_norm, mla_qr_norm, mla_kn_norm, mla_kr_norm, s5_lam_re, s5_lam_im, s5_log_step, s5_b_re, s5_b_im, s5_c_re, s5_c_im, s5_d, s5_w_glu, s5_b_glu, w_out, moe_w_router, moe_b_router, moe_w_gu, moe_b_gu, moe_w_dn, moe_b_dn):
    batch, n_l, _ = x.shape
    n_c = ctx.shape[1]
    depth = w_in.shape[0]
    params = dict(ada_dn=ada_dn, ada_up=ada_up, ada_b=ada_b, norm1=norm1, norm2=norm2, w_in=w_in,
                  gdn_conv=gdn_conv, gdn_a_log=gdn_a_log, gdn_dt_bias=gdn_dt_bias, gdn_norm=gdn_norm,
                  mla_q_norm=mla_q_norm, mla_kv_norm=mla_kv_norm, mla_w_uq=mla_w_uq, mla_w_ukv=mla_w_ukv,
                  mla_qn_norm=mla_qn_norm, mla_qr_norm=mla_qr_norm, mla_kn_norm=mla_kn_norm,
                  mla_kr_norm=mla_kr_norm, s5_lam_re=s5_lam_re, s5_lam_im=s5_lam_im,
                  s5_log_step=s5_log_step, s5_b_re=s5_b_re, s5_b_im=s5_b_im, s5_c_re=s5_c_re,
                  s5_c_im=s5_c_im, s5_d=s5_d, s5_w_glu=s5_w_glu, s5_b_glu=s5_b_glu, w_out=w_out,
                  moe_w_router=moe_w_router, moe_b_router=moe_b_router, moe_w_gu=moe_w_gu,
                  moe_b_gu=moe_b_gu, moe_w_dn=moe_w_dn, moe_b_dn=moe_b_dn)
    rope_tabs = _rope_tables(n_l)
    xl = x.reshape(batch * n_l, D_MODEL).astype(F32)
    xc = ctx.reshape(batch * n_c, D_MODEL).astype(F32)
    cvecs = jnp.concatenate([c.astype(F32), c_ctx.astype(F32)[None, :]], axis=0)
    for layer in range(depth):
        lp = {name: val[layer] for name, val in params.items()}
        mods = _ada_mods(cvecs, lp)
        xl, xc = _layer(xl, xc, mods[:batch * N_MOD], mods[batch * N_MOD:], lp, rope_tabs, batch, n_l, n_c,
                        layer == depth - 1)
    return xl.reshape(batch, n_l, D_MODEL).astype(x.dtype)
```

```python
import functools
import math

import jax
import jax.numpy as jnp
from jax import lax
from jax.experimental import pallas as pl
from jax.experimental.pallas import tpu as pltpu

F32 = jnp.float32
BF16 = jnp.bfloat16
HIGHEST = lax.Precision.HIGHEST

D_MODEL = 4096
GRID_W = 64
GDN_HEADS = 12
GDN_DK = 128
GDN_DV = 128
GDN_CONV = 5
GDN_CHUNK = 64
MLA_HEADS = 12
MLA_Q_RANK = 1024
MLA_KV_RANK = 512
MLA_NOPE = 128
MLA_ROPE = 64
MLA_V = 128
MLA_SCALE = (MLA_NOPE + MLA_ROPE) ** -0.5
ROPE_THETA = 10000.0
S5_WIDTH = 1024
S5_GROUP = 16
S5_GROUPS = S5_WIDTH // S5_GROUP
S5_STATE = 64
GDN_QK = GDN_HEADS * GDN_DK
GDN_VW = GDN_HEADS * GDN_DV
GDN_NQKV = 2 * GDN_QK + GDN_VW
N_MOD = 6
N_EXPERTS = 32
TOP_K = 4
EXPERT_FF = 384
SWIGLU_LIMIT = 7.0
SWIGLU_ALPHA = 1.702
RMS_EPS = 1e-6

LANES = 128
SUBLANES = 8
VMEM_LIMIT_BYTES = 56 * 1024 * 1024

MLA_QP = 2 * LANES
COL_QKV = 0
COL_Z = GDN_NQKV
COL_MQ = COL_Z + GDN_VW
COL_S5 = COL_MQ + MLA_Q_RANK
COL_MKV = COL_S5 + S5_WIDTH
COL_BG = COL_MKV + MLA_KV_RANK
COL_KR = COL_BG + LANES
N_IN_PAD = COL_KR + LANES

S5_SLABS = S5_WIDTH // LANES
S5_SLAB_STATE = (S5_GROUPS // S5_SLABS) * S5_STATE


def _cparams(sem):
    return pltpu.CompilerParams(dimension_semantics=sem, vmem_limit_bytes=VMEM_LIMIT_BYTES)


def _tile(n, pref):
    if n <= pref:
        return n
    t = pref
    while n % t:
        t -= 1
    return t


def _silu(x):
    return x * jax.nn.sigmoid(x)


def _dot(a, b, dims=(((1,), (0,)), ((), ())), exact=False):
    if exact:
        return lax.dot_general(a.astype(F32), b.astype(F32), dims, precision=HIGHEST,
                               preferred_element_type=F32)
    return lax.dot_general(a.astype(BF16), b.astype(BF16), dims, preferred_element_type=F32)


_NT = (((1,), (1,)), ((), ()))
_TN = (((0,), (0,)), ((), ()))


def _mm_kernel(*refs, prologue, exact, has_bias, has_gain):
    it = iter(refs)
    a_ref = next(it)
    w_ref = next(it)
    b_ref = next(it) if has_bias else None
    g_ref = next(it) if has_gain else None
    o_ref = next(it)
    a = a_ref[...]
    if prologue == "silu":
        a = _silu(a.astype(F32))
    elif prologue == "rms":
        af = a.astype(F32)
        a = af * lax.rsqrt(jnp.mean(af * af, axis=-1, keepdims=True) + RMS_EPS) * g_ref[...]
    acc = _dot(a, w_ref[...], exact=exact)
    if has_bias:
        acc = acc + b_ref[...]
    o_ref[...] = acc.astype(o_ref.dtype)


def _mm(a, w, *, bias=None, gain=None, prologue=None, exact=False, out_dtype=F32,
        tm=1024, tn=1024, a_colblk=0, name="mm"):
    m = a.shape[0]
    k, n = w.shape
    tm = _tile(m, tm)
    tn = _tile(n, tn)
    in_specs = [pl.BlockSpec((tm, k), lambda i, j: (i, a_colblk)),
                pl.BlockSpec((k, tn), lambda i, j: (0, j))]
    args = [a, w]
    if bias is not None:
        in_specs.append(pl.BlockSpec((1, tn), lambda i, j: (0, j)))
        args.append(bias.reshape(1, n).astype(F32))
    if gain is not None:
        in_specs.append(pl.BlockSpec((1, k), lambda i, j: (0, 0)))
        args.append(gain.reshape(1, k).astype(F32))
    return pl.pallas_call(
        functools.partial(_mm_kernel, prologue=prologue, exact=exact,
                          has_bias=bias is not None, has_gain=gain is not None),
        out_shape=jax.ShapeDtypeStruct((m, n), out_dtype),
        grid=(m // tm, n // tn),
        in_specs=in_specs,
        out_specs=pl.BlockSpec((tm, tn), lambda i, j: (i, j)),
        compiler_params=_cparams(("parallel", "arbitrary")),
        name=name,
    )(*args)


def _norm_mod_kernel(x_ref, g_ref, sh_ref, sc_ref, o_ref):
    x = x_ref[...]
    y = x * lax.rsqrt(jnp.mean(x * x, axis=-1, keepdims=True) + RMS_EPS) * g_ref[...]
    o_ref[...] = (y * (1.0 + sc_ref[...]) + sh_ref[...]).astype(o_ref.dtype)


def _mod_spec(which, rows_per_mod, tm, tn=None):
    width = D_MODEL if tn is None else tn
    per_tile = rows_per_mod // tm
    if tn is None:
        return pl.BlockSpec((None, 1, width), lambda i: ((i // per_tile) * N_MOD + which, 0, 0))
    return pl.BlockSpec((None, 1, width), lambda i, j: ((i // per_tile) * N_MOD + which, 0, j))


def _norm_mod(x, gain, mods, which_shift, which_scale, rows_per_mod):
    m = x.shape[0]
    tm = _tile(math.gcd(m, rows_per_mod), 512)
    return pl.pallas_call(
        _norm_mod_kernel,
        out_shape=jax.ShapeDtypeStruct((m, D_MODEL), BF16),
        grid=(m // tm,),
        in_specs=[pl.BlockSpec((tm, D_MODEL), lambda i: (i, 0)),
                  pl.BlockSpec((1, D_MODEL), lambda i: (0, 0)),
                  _mod_spec(which_shift, rows_per_mod, tm),
                  _mod_spec(which_scale, rows_per_mod, tm)],
        out_specs=pl.BlockSpec((tm, D_MODEL), lambda i: (i, 0)),
        compiler_params=_cparams(("parallel",)),
        name="norm_mod",
    )(x, gain.reshape(1, D_MODEL), mods, mods)


def _norm_route_kernel(x_ref, g_ref, sh_ref, sc_ref, wr_ref, br_ref, h_ref, idx_ref, wt_ref):
    x = x_ref[...]
    y = x * lax.rsqrt(jnp.mean(x * x, axis=-1, keepdims=True) + RMS_EPS) * g_ref[...]
    h = y * (1.0 + sc_ref[...]) + sh_ref[...]
    h_ref[...] = h.astype(h_ref.dtype)
    logits = _dot(h, wr_ref[...], exact=True) + br_ref[...]
    lane = lax.broadcasted_iota(jnp.int32, logits.shape, 1).astype(F32)
    neg = jnp.float32(-jnp.inf)
    cur = jnp.where(lane < N_EXPERTS, logits, neg)
    idx_out = jnp.zeros(logits.shape, F32)
    val_out = jnp.full(logits.shape, neg, F32)
    for kk in range(TOP_K):
        mx = jnp.max(cur, axis=-1, keepdims=True)
        sel = jnp.min(jnp.where(cur == mx, lane, float(LANES)), axis=-1, keepdims=True)
        idx_out = jnp.where(lane == kk, sel, idx_out)
        val_out = jnp.where(lane == kk, mx, val_out)
        cur = jnp.where(lane == sel, neg, cur)
    top = jnp.max(val_out, axis=-1, keepdims=True)
    e = jnp.exp(val_out - top)
    wt_ref[...] = e / jnp.sum(e, axis=-1, keepdims=True)
    idx_ref[...] = idx_out.astype(jnp.int32)


def _norm_route(x, gain, mods, which_shift, which_scale, rows_per_mod, w_router, b_router):
    m = x.shape[0]
    tm = _tile(math.gcd(m, rows_per_mod), 512)
    row = lambda i: (i, 0)
    return pl.pallas_call(
        _norm_route_kernel,
        out_shape=(jax.ShapeDtypeStruct((m, D_MODEL), BF16),
                   jax.ShapeDtypeStruct((m, LANES), jnp.int32),
                   jax.ShapeDtypeStruct((m, LANES), F32)),
        grid=(m // tm,),
        in_specs=[pl.BlockSpec((tm, D_MODEL), row),
                  pl.BlockSpec((1, D_MODEL), lambda i: (0, 0)),
                  _mod_spec(which_shift, rows_per_mod, tm),
                  _mod_spec(which_scale, rows_per_mod, tm),
                  pl.BlockSpec((D_MODEL, LANES), lambda i: (0, 0)),
                  pl.BlockSpec((1, LANES), lambda i: (0, 0))],
        out_specs=(pl.BlockSpec((tm, D_MODEL), row),
                   pl.BlockSpec((tm, LANES), row),
                   pl.BlockSpec((tm, LANES), row)),
        compiler_params=_cparams(("parallel",)),
        name="norm_route",
    )(x, gain.reshape(1, D_MODEL), mods, mods, w_router, b_router)


def _gdn_prep_kernel(p_ref, w_ref, o_ref, *, n_qk_blocks, n_q_blocks, heads_per_block):
    j = pl.program_id(1)
    x = p_ref[...]
    n = x.shape[0]
    w = w_ref[...]
    t = lax.broadcasted_iota(jnp.int32, x.shape, 0)
    acc = x * w[GDN_CONV // 2:GDN_CONV // 2 + 1, :]
    for tap in range(GDN_CONV):
        off = tap - GDN_CONV // 2
        if off == 0:
            continue
        shifted = pltpu.roll(x, (-off) % n, 0)
        valid = (t + off >= 0) & (t + off < n)
        acc = acc + jnp.where(valid, shifted, 0.0) * w[tap:tap + 1, :]
    y = _silu(acc)

    @pl.when(j < n_qk_blocks)
    def _():
        scale = jnp.where(j < n_q_blocks, jnp.float32(GDN_DK ** -0.5), jnp.float32(1.0))
        for hh in range(heads_per_block):
            yh = y[:, hh * GDN_DK:(hh + 1) * GDN_DK]
            r = lax.rsqrt(jnp.sum(yh * yh, axis=-1, keepdims=True) + RMS_EPS)
            o_ref[:, hh * GDN_DK:(hh + 1) * GDN_DK] = yh * (r * scale)

    @pl.when(j >= n_qk_blocks)
    def _():
        o_ref[...] = y


def _gdn_prep(p, conv_w, batch, seg):
    hpb = 4
    wblk = hpb * GDN_DK
    return pl.pallas_call(
        functools.partial(_gdn_prep_kernel, n_qk_blocks=2 * GDN_QK // wblk,
                          n_q_blocks=GDN_QK // wblk, heads_per_block=hpb),
        out_shape=jax.ShapeDtypeStruct((batch * seg, GDN_NQKV), F32),
        grid=(batch, GDN_NQKV // wblk),
        in_specs=[pl.BlockSpec((seg, wblk), lambda b, j: (b, j)),
                  pl.BlockSpec((GDN_CONV, wblk), lambda b, j: (0, j))],
        out_specs=pl.BlockSpec((seg, wblk), lambda b, j: (b, j)),
        compiler_params=_cparams(("parallel", "arbitrary")),
        name="gdn_prep",
    )(p, conv_w)


def _dot_split(x, y):
    xh = x.astype(BF16)
    xl = (x - xh.astype(F32)).astype(BF16)
    yh = y.astype(BF16)
    yl = (y - yh.astype(F32)).astype(BF16)
    return _dot(xh, yh) + _dot(xl, yh) + _dot(xh, yl)


def _unit_tri_inverse_many(a_list, eye, blk16, blk32):
    mm = lambda xs, ys: [_dot(x, y) for x, y in zip(xs, ys)]
    n1 = [-jnp.where(blk16, a, 0.0) for a in a_list]
    n2 = mm(n1, n1)
    n4 = mm(n2, n2)
    n8 = mm(n4, n4)
    p = mm([eye + x for x in n1], [eye + x for x in n2])
    p = mm(p, [eye + x for x in n4])
    p = mm(p, [eye + x for x in n8])
    a1 = [jnp.where(blk32 & jnp.logical_not(blk16), a, 0.0) for a in a_list]
    t = mm(a1, p)
    p = [x - y for x, y in zip(p, mm(p, t))]
    a2 = [jnp.where(blk32, 0.0, a) for a in a_list]
    t = mm(a2, p)
    p = [x - y for x, y in zip(p, mm(p, t))]
    resid = [(eye - x) - _dot_split(a, x) for a, x in zip(a_list, p)]
    return [x + y for x, y in zip(p, mm(p, resid))]


def _gdn_kernel(q_ref, k_ref, v_ref, sc_ref, sct_ref, s0_ref, o_ref, sout_ref,
                s_scr, wq_scr, u_scr, kd_scr, qk_scr, eg_scr, *, heads_per_block, n_chunks):
    hpb = heads_per_block
    hb = pl.program_id(1)
    d = pl.program_id(2)
    s = pl.program_id(3)
    cs = GDN_CHUNK

    @pl.when(s == 0)
    def _():
        s_scr[...] = s0_ref[...]

    sgn = 1 - 2 * d
    ii = lax.broadcasted_iota(jnp.int32, (cs, cs), 0)
    jj = lax.broadcasted_iota(jnp.int32, (cs, cs), 1)
    rel = (ii - jj) * sgn
    incl = rel >= 0
    strict = rel > 0
    eye = jnp.where(ii == jj, 1.0, 0.0).astype(F32)
    blk16 = (ii // 16) == (jj // 16)
    blk32 = (ii // 32) == (jj // 32)
    lane = lax.broadcasted_iota(jnp.int32, (cs, LANES), 1)
    neg = jnp.float32(-jnp.inf)

    beta, gc, gr, gl, q, k, v = [], [], [], [], [], [], []
    for c in range(n_chunks):
        rows = slice(c * cs, (c + 1) * cs)
        sc = sc_ref[rows, :]
        for hh in range(hpb):
            head = hb * hpb + hh
            i_beta = d * GDN_HEADS + head
            i_g = 2 * GDN_HEADS + i_beta
            beta.append(jnp.sum(jnp.where(lane == i_beta, sc, 0.0), axis=1, keepdims=True))
            gc.append(jnp.sum(jnp.where(lane == i_g, sc, 0.0), axis=1, keepdims=True))
            g_row = sct_ref[c, pl.ds(i_g, 1), :]
            gr.append(g_row)
            gl.append(jnp.where(d == 0, g_row[:, cs - 1:cs], g_row[:, 0:1]))
            cols = slice(hh * GDN_DK, (hh + 1) * GDN_DK)
            q.append(q_ref[rows, cols])
            k.append(k_ref[rows, cols])
            v.append(v_ref[rows, cols])
    probs = range(n_chunks * hpb)
    decay = [jnp.exp(jnp.where(incl, gc[i] - gr[i], neg)) for i in probs]
    kb = [k[i] * beta[i] for i in probs]
    kq = [_dot(jnp.concatenate([kb[i], q[i]], axis=0), k[i], _NT) for i in probs]
    a = [jnp.where(strict, kq[i][:cs] * decay[i], 0.0) for i in probs]
    qk = [kq[i][cs:] * decay[i] for i in probs]
    tinv = _unit_tri_inverse_many(a, eye, blk16, blk32)
    eg = [jnp.exp(gc[i]) for i in probs]
    uw = [_dot(tinv[i], jnp.concatenate([v[i] * beta[i], kb[i] * eg[i]], axis=1)) for i in probs]
    for i in probs:
        u_scr[i] = uw[i][:, :GDN_DV]
        wq_scr[i, :cs, :] = uw[i][:, GDN_DV:].astype(wq_scr.dtype)
        wq_scr[i, cs:, :] = (q[i] * eg[i]).astype(wq_scr.dtype)
        kd_scr[i] = (k[i] * jnp.exp(gl[i] - gc[i])).astype(kd_scr.dtype)
        qk_scr[i] = qk[i].astype(qk_scr.dtype)
        eg_scr[i] = jnp.broadcast_to(jnp.exp(gl[i]), (SUBLANES, LANES))

    def chunk(ci, carry):
        c = jnp.where(d == 0, ci, n_chunks - 1 - ci)
        r0 = pl.multiple_of(c * cs, cs)
        heads = range(hpb)
        st = [s_scr[i] for i in heads]
        ws = [_dot(wq_scr[c * hpb + i], st[i]) for i in heads]
        v_new = [u_scr[c * hpb + i] - ws[i][:cs] for i in heads]
        o = [ws[i][cs:] + _dot(qk_scr[c * hpb + i], v_new[i]) for i in heads]
        st_new = [st[i] * eg_scr[c * hpb + i][0:1, 0:1] + _dot(kd_scr[c * hpb + i], v_new[i], _TN)
                  for i in heads]
        for i in heads:
            o_ref[pl.ds(r0, cs), i * GDN_DV:(i + 1) * GDN_DV] = o[i]
            s_scr[i] = st_new[i]
        return carry

    lax.fori_loop(0, n_chunks, chunk, 0, unroll=True)

    @pl.when(s == pl.num_programs(3) - 1)
    def _():
        sout_ref[...] = s_scr[...]


def _gdn_scan(qkv, scal, scal_t, s0, batch, seg):
    hpb = 6
    wblk = hpb * GDN_DK
    tb = _tile(seg, 256)
    nblk = seg // tb
    nch = tb // GDN_CHUNK
    n_hb = GDN_HEADS // hpb
    k_off = GDN_QK // wblk
    v_off = 2 * GDN_QK // wblk
    n_prob = nch * hpb

    def rowblk(b, d, s):
        return b * nblk + jnp.where(d == 0, s, nblk - 1 - s)

    state_spec = pl.BlockSpec((None, None, hpb, GDN_DK, GDN_DV), lambda b, h, d, s: (d, b, h, 0, 0))
    return pl.pallas_call(
        functools.partial(_gdn_kernel, heads_per_block=hpb, n_chunks=nch),
        out_shape=(jax.ShapeDtypeStruct((2, batch * seg, GDN_VW), F32),
                   jax.ShapeDtypeStruct(s0.shape, F32)),
        grid=(batch, n_hb, 2, nblk),
        in_specs=[pl.BlockSpec((tb, wblk), lambda b, h, d, s: (rowblk(b, d, s), h)),
                  pl.BlockSpec((tb, wblk), lambda b, h, d, s: (rowblk(b, d, s), k_off + h)),
                  pl.BlockSpec((tb, wblk), lambda b, h, d, s: (rowblk(b, d, s), v_off + h)),
                  pl.BlockSpec((tb, LANES), lambda b, h, d, s: (rowblk(b, d, s), 0)),
                  pl.BlockSpec((nch, LANES, GDN_CHUNK), lambda b, h, d, s: (rowblk(b, d, s), 0, 0)),
                  state_spec],
        out_specs=(pl.BlockSpec((None, tb, wblk), lambda b, h, d, s: (d, rowblk(b, d, s), h)),
                   state_spec),
        scratch_shapes=[pltpu.VMEM((hpb, GDN_DK, GDN_DV), F32),
                        pltpu.VMEM((n_prob, 2 * GDN_CHUNK, GDN_DK), BF16),
                        pltpu.VMEM((n_prob, GDN_CHUNK, GDN_DV), F32),
                        pltpu.VMEM((n_prob, GDN_CHUNK, GDN_DK), BF16),
                        pltpu.VMEM((n_prob, GDN_CHUNK, GDN_CHUNK), BF16),
                        pltpu.VMEM((n_prob, SUBLANES, LANES), F32)],
        compiler_params=_cparams(("parallel", "parallel", "parallel", "arbitrary")),
        name="gdn_scan",
    )(qkv, qkv, qkv, scal, scal_t, s0)


def _gdn_scalars(p, a_log, dt_bias):
    r = p.shape[0]
    nh = GDN_HEADS
    bg = p[:, COL_BG:COL_BG + 4 * nh]
    beta = jax.nn.sigmoid(bg[:, :2 * nh])
    g = -jnp.exp(a_log.astype(F32)).reshape(1, 2 * nh) * jax.nn.softplus(
        bg[:, 2 * nh:] + dt_bias.astype(F32).reshape(1, 2 * nh))
    gch = g.reshape(r // GDN_CHUNK, GDN_CHUNK, 2 * nh)
    gc_f = jnp.cumsum(gch[..., :nh], axis=1)
    gc_b = jnp.flip(jnp.cumsum(jnp.flip(gch[..., nh:], axis=1), axis=1), axis=1)
    gc = jnp.concatenate([gc_f, gc_b], axis=-1).reshape(r, 2 * nh)
    scal = jnp.concatenate([beta, gc, jnp.zeros((r, LANES - 4 * nh), F32)], axis=1)
    scal_t = jnp.transpose(scal.reshape(r // GDN_CHUNK, GDN_CHUNK, LANES), (0, 2, 1))
    return scal, scal_t


def _gdn_out_kernel(o_ref, z_ref, g_ref, y_ref):
    o = o_ref[0] + o_ref[1]
    z = z_ref[...]
    gain = g_ref[...]
    for h in range(GDN_HEADS):
        cols = slice(h * GDN_DV, (h + 1) * GDN_DV)
        oh = o[:, cols]
        yh = oh * lax.rsqrt(jnp.mean(oh * oh, axis=-1, keepdims=True) + RMS_EPS) * gain
        y_ref[:, cols] = (yh * _silu(z[:, cols])).astype(y_ref.dtype)


def _gdn_out(o2, p, gain):
    m = p.shape[0]
    tm = _tile(m, 256)
    return pl.pallas_call(
        _gdn_out_kernel,
        out_shape=jax.ShapeDtypeStruct((m, GDN_VW), BF16),
        grid=(m // tm,),
        in_specs=[pl.BlockSpec((2, tm, GDN_VW), lambda i: (0, i, 0)),
                  pl.BlockSpec((tm, GDN_VW), lambda i: (i, COL_Z // GDN_VW)),
                  pl.BlockSpec((1, GDN_DV), lambda i: (0, 0))],
        out_specs=pl.BlockSpec((tm, GDN_VW), lambda i: (i, 0)),
        compiler_params=_cparams(("parallel",)),
        name="gdn_out",
    )(o2, p, gain.reshape(1, GDN_DV))


def _mla_prep_kernel(q_ref, kv_ref, kr_ref, cos_ref, sin_ref, gq_ref, gqr_ref, gk_ref, gkr_ref,
                     qo_ref, ko_ref, vo_ref, *, rope):
    lane = lax.broadcasted_iota(jnp.int32, (q_ref.shape[0], LANES), 1)

    def rms(x, g, width):
        return x * lax.rsqrt(jnp.sum(x * x, axis=-1, keepdims=True) * (1.0 / width) + RMS_EPS) * g

    def rot(x):
        if not rope:
            return x
        half = MLA_ROPE // 2
        swapped = jnp.where(lane < half, pltpu.roll(x, LANES - half, 1), pltpu.roll(x, half, 1))
        return x * cos_ref[...] + swapped * sin_ref[...]

    q = q_ref[...]
    kv = kv_ref[...]
    qn = rms(q[:, :MLA_NOPE], gq_ref[...], MLA_NOPE)
    qr = rot(rms(q[:, MLA_NOPE:], gqr_ref[...], MLA_ROPE))
    qo_ref[:, :MLA_NOPE] = (qn * MLA_SCALE).astype(qo_ref.dtype)
    qo_ref[:, MLA_NOPE:] = (qr * MLA_SCALE).astype(qo_ref.dtype)
    kn = rms(kv[:, :MLA_NOPE], gk_ref[...], MLA_NOPE)
    kr = rot(rms(kr_ref[...], gkr_ref[...], MLA_ROPE))
    ko_ref[:, :MLA_NOPE] = kn.astype(ko_ref.dtype)
    ko_ref[:, MLA_NOPE:] = kr.astype(ko_ref.dtype)
    vo_ref[...] = kv[:, MLA_NOPE:].astype(vo_ref.dtype)


def _mla_prep(q, kv, p, cos_t, sin_t, gains, seg, rope):
    m = q.shape[0]
    tm = _tile(seg, 512)
    tpb = seg // tm
    gq, gqr, gk, gkr = gains
    vec = pl.BlockSpec((1, LANES), lambda i, h: (0, 0))
    tab = pl.BlockSpec((tm, LANES), lambda i, h: (i % tpb, 0))
    return pl.pallas_call(
        functools.partial(_mla_prep_kernel, rope=rope),
        out_shape=(jax.ShapeDtypeStruct((m, MLA_HEADS * MLA_QP), BF16),
                   jax.ShapeDtypeStruct((m, MLA_HEADS * MLA_QP), BF16),
                   jax.ShapeDtypeStruct((m, MLA_HEADS * MLA_V), BF16)),
        grid=(m // tm, MLA_HEADS),
        in_specs=[pl.BlockSpec((tm, MLA_QP), lambda i, h: (i, h)),
                  pl.BlockSpec((tm, MLA_NOPE + MLA_V), lambda i, h: (i, h)),
                  pl.BlockSpec((tm, LANES), lambda i, h: (i, COL_KR // LANES)),
                  tab, tab, vec, vec, vec, vec],
        out_specs=(pl.BlockSpec((tm, MLA_QP), lambda i, h: (i, h)),
                   pl.BlockSpec((tm, MLA_QP), lambda i, h: (i, h)),
                   pl.BlockSpec((tm, MLA_V), lambda i, h: (i, h))),
        compiler_params=_cparams(("parallel", "arbitrary")),
        name="mla_prep",
    )(q, kv, p, cos_t, sin_t, gq, gqr, gk, gkr)


def _attn_kernel(*refs, with_latent):
    if with_latent:
        q_ref, kc_ref, vc_ref, kl_ref, vl_ref, o_ref = refs
    else:
        q_ref, kc_ref, vc_ref, o_ref = refs
    q = q_ref[...]
    s_c = _dot(q, kc_ref[...], _NT)
    m = jnp.max(s_c, axis=-1, keepdims=True)
    if with_latent:
        s_l = _dot(q, kl_ref[...], _NT)
        m = jnp.maximum(m, jnp.max(s_l, axis=-1, keepdims=True))
    p_c = jnp.exp(s_c - m)
    den = jnp.sum(p_c, axis=-1, keepdims=True)
    acc = _dot(p_c, vc_ref[...])
    if with_latent:
        p_l = jnp.exp(s_l - m)
        den = den + jnp.sum(p_l, axis=-1, keepdims=True)
        acc = acc + _dot(p_l, vl_ref[...])
    o_ref[...] = (acc / den).astype(o_ref.dtype)


def _attention(qh, kh_c, vh_c, kh_l, vh_l, batch, n_q, n_c, n_l):
    with_latent = kh_l is not None
    tq = _tile(n_q, 512)
    nqt = n_q // tq
    in_specs = [pl.BlockSpec((tq, MLA_QP), lambda b, h, i: (b * nqt + i, h)),
                pl.BlockSpec((n_c, MLA_QP), lambda b, h, i: (b, h)),
                pl.BlockSpec((n_c, MLA_V), lambda b, h, i: (b, h))]
    args = [qh, kh_c, vh_c]
    if with_latent:
        in_specs += [pl.BlockSpec((n_l, MLA_QP), lambda b, h, i: (b, h)),
                     pl.BlockSpec((n_l, MLA_V), lambda b, h, i: (b, h))]
        args += [kh_l, vh_l]
    return pl.pallas_call(
        functools.partial(_attn_kernel, with_latent=with_latent),
        out_shape=jax.ShapeDtypeStruct((batch * n_q, MLA_HEADS * MLA_V), BF16),
        grid=(batch, MLA_HEADS, nqt),
        in_specs=in_specs,
        out_specs=pl.BlockSpec((tq, MLA_V), lambda b, h, i: (b * nqt + i, h)),
        compiler_params=_cparams(("parallel", "parallel", "arbitrary")),
        name="mla_attn",
    )(*args)


S5_SLABS_PER_STEP = 2


def _s5_kernel(u_ref, w_ref, a_ref, c_ref, h0_ref, y_ref, hout_ref, buf, h_scr, *, steps, batch):
    d = pl.program_id(0)
    s = pl.program_id(2)
    ns = S5_SLAB_STATE
    slabs = range(S5_SLABS_PER_STEP)

    @pl.when(s == 0)
    def _():
        h_scr[...] = h0_ref[...]

    u2 = u_ref[...].reshape(steps * batch, S5_SLABS_PER_STEP * LANES)
    for j in slabs:
        buf[j] = _dot(u2[:, j * LANES:(j + 1) * LANES], w_ref[j])
    ar = [jnp.broadcast_to(a_ref[j, 0:1, :], (batch, ns)) for j in slabs]
    ai = [jnp.broadcast_to(a_ref[j, 1:2, :], (batch, ns)) for j in slabs]

    def step(i, carry):
        t = jnp.where(d == 0, i, steps - 1 - i)
        r0 = pl.multiple_of(t * batch, batch)
        out = []
        for j in slabs:
            hr, hi = carry[2 * j], carry[2 * j + 1]
            nr = ar[j] * hr - ai[j] * hi + buf[j, pl.ds(r0, batch), :ns]
            ni = ar[j] * hi + ai[j] * hr + buf[j, pl.ds(r0, batch), ns:]
            buf[j, pl.ds(r0, batch), :ns] = nr
            buf[j, pl.ds(r0, batch), ns:] = ni
            out += [nr, ni]
        return tuple(out)

    init = []
    for j in slabs:
        init += [h_scr[j, :, :ns], h_scr[j, :, ns:]]
    fin = lax.fori_loop(0, steps, step, tuple(init))
    for j in slabs:
        h_scr[j, :, :ns] = fin[2 * j]
        h_scr[j, :, ns:] = fin[2 * j + 1]
        y_ref[:, :, j * LANES:(j + 1) * LANES] = _dot(buf[j], c_ref[j]).reshape(steps, batch, LANES)

    @pl.when(s == pl.num_programs(2) - 1)
    def _():
        hout_ref[...] = h_scr[...]


def _s5_scan(u_t, w_bu, a_pack, c_pack, h0):
    seg, batch, _ = u_t.shape
    steps = _tile(seg, 128)
    nblk = seg // steps
    spb = S5_SLABS_PER_STEP

    def tblk(d, s):
        return jnp.where(d == 0, s, nblk - 1 - s)

    st_spec = pl.BlockSpec((None, spb, batch, 2 * S5_SLAB_STATE), lambda d, j, s: (d, j, 0, 0))
    return pl.pallas_call(
        functools.partial(_s5_kernel, steps=steps, batch=batch),
        out_shape=(jax.ShapeDtypeStruct((2, seg, batch, S5_WIDTH), F32),
                   jax.ShapeDtypeStruct(h0.shape, F32)),
        grid=(2, S5_SLABS // spb, nblk),
        in_specs=[pl.BlockSpec((steps, batch, spb * LANES), lambda d, j, s: (tblk(d, s), 0, j)),
                  pl.BlockSpec((None, spb, LANES, 2 * S5_SLAB_STATE), lambda d, j, s: (d, j, 0, 0)),
                  pl.BlockSpec((None, spb, 2, S5_SLAB_STATE), lambda d, j, s: (d, j, 0, 0)),
                  pl.BlockSpec((None, spb, 2 * S5_SLAB_STATE, LANES), lambda d, j, s: (d, j, 0, 0)),
                  st_spec],
        out_specs=(pl.BlockSpec((None, steps, batch, spb * LANES), lambda d, j, s: (d, tblk(d, s), 0, j)),
                   st_spec),
        scratch_shapes=[pltpu.VMEM((spb, steps * batch, 2 * S5_SLAB_STATE), F32),
                        pltpu.VMEM((spb, batch, 2 * S5_SLAB_STATE), F32)],
        compiler_params=_cparams(("parallel", "parallel", "arbitrary")),
        name="s5_scan",
    )(u_t, w_bu, a_pack, c_pack, h0)


def _s5_glu_kernel(y_ref, u_ref, d_ref, w_ref, b_ref, o_ref):
    y = y_ref[0] + y_ref[1] + d_ref[...] * u_ref[...]
    z = _dot(jax.nn.gelu(y), w_ref[...]) + b_ref[...]
    o_ref[...] = (z[:, :S5_WIDTH] * jax.nn.sigmoid(z[:, S5_WIDTH:])).astype(o_ref.dtype)


def _s5_glu(y2, p, d_skip, w_glu, b_glu, batch, seg):
    tl = _tile(seg, 512)
    nt = seg // tl
    return pl.pallas_call(
        _s5_glu_kernel,
        out_shape=jax.ShapeDtypeStruct((batch * seg, S5_WIDTH), BF16),
        grid=(batch, nt),
        in_specs=[pl.BlockSpec((2, tl, S5_WIDTH), lambda b, i: (0, i, b)),
                  pl.BlockSpec((tl, S5_WIDTH), lambda b, i: (b * nt + i, COL_S5 // S5_WIDTH)),
                  pl.BlockSpec((1, S5_WIDTH), lambda b, i: (0, 0)),
                  pl.BlockSpec((S5_WIDTH, 2 * S5_WIDTH), lambda b, i: (0, 0)),
                  pl.BlockSpec((1, 2 * S5_WIDTH), lambda b, i: (0, 0))],
        out_specs=pl.BlockSpec((tl, S5_WIDTH), lambda b, i: (b * nt + i, 0)),
        compiler_params=_cparams(("parallel", "arbitrary")),
        name="s5_glu",
    )(y2, p, d_skip.reshape(1, S5_WIDTH), w_glu, b_glu.reshape(1, 2 * S5_WIDTH))


def _s5_params(lp):
    gps = S5_GROUPS // S5_SLABS
    eye = jnp.eye(gps, dtype=F32)
    w_l, a_l, c_l = [], [], []
    for d in range(2):
        lam_re = lp["s5_lam_re"][d].astype(F32)
        lam_im = lp["s5_lam_im"][d].astype(F32)
        dt = jnp.exp(lp["s5_log_step"][d].astype(F32))[:, None]
        mag = jnp.exp(lam_re * dt)
        ang = lam_im * dt
        a_re, a_im = mag * jnp.cos(ang), mag * jnp.sin(ang)
        den = lam_re * lam_re + lam_im * lam_im
        nr, ni = a_re - 1.0, a_im
        coef_re = ((nr * lam_re + ni * lam_im) / den)[..., None]
        coef_im = ((ni * lam_re - nr * lam_im) / den)[..., None]
        b_re, b_im = lp["s5_b_re"][d].astype(F32), lp["s5_b_im"][d].astype(F32)
        bb_re = coef_re * b_re - coef_im * b_im
        bb_im = coef_re * b_im + coef_im * b_re

        def bdiag_in(bb):
            x = bb.reshape(S5_SLABS, gps, S5_STATE, S5_GROUP)
            return jnp.einsum("jgph,gk->jghkp", x, eye).reshape(S5_SLABS, LANES, S5_SLAB_STATE)

        def bdiag_out(cc):
            x = cc.astype(F32).reshape(S5_SLABS, gps, S5_GROUP, S5_STATE)
            return jnp.einsum("jghp,gk->jgpkh", x, eye).reshape(S5_SLABS, S5_SLAB_STATE, LANES)

        w_l.append(jnp.concatenate([bdiag_in(bb_re), bdiag_in(bb_im)], axis=-1))
        a_l.append(jnp.stack([a_re.reshape(S5_SLABS, S5_SLAB_STATE),
                              a_im.reshape(S5_SLABS, S5_SLAB_STATE)], axis=1))
        c_l.append(jnp.concatenate([bdiag_out(lp["s5_c_re"][d]), -bdiag_out(lp["s5_c_im"][d])], axis=1))
    return jnp.stack(w_l).astype(BF16), jnp.stack(a_l), jnp.stack(c_l).astype(BF16)


def _wout_kernel(a1_ref, a2_ref, a3_ref, w1_ref, w2_ref, w3_ref, x_ref, g_ref, o_ref):
    acc = _dot(a1_ref[...], w1_ref[...])
    acc = acc + _dot(a2_ref[...], w2_ref[...])
    acc = acc + _dot(a3_ref[...], w3_ref[...])
    o_ref[...] = x_ref[...] + g_ref[...] * acc


def _wout(a1, a2, a3, w1, w2, w3, x, mods, which_gate, rows_per_mod):
    m = x.shape[0]
    tm = _tile(math.gcd(m, rows_per_mod), 1024)
    tn = 1024
    row = lambda i, j: (i, 0)
    col = lambda i, j: (0, j)
    return pl.pallas_call(
        _wout_kernel,
        out_shape=jax.ShapeDtypeStruct((m, D_MODEL), F32),
        grid=(m // tm, D_MODEL // tn),
        in_specs=[pl.BlockSpec((tm, a1.shape[1]), row),
                  pl.BlockSpec((tm, a2.shape[1]), row),
                  pl.BlockSpec((tm, a3.shape[1]), row),
                  pl.BlockSpec((w1.shape[0], tn), col),
                  pl.BlockSpec((w2.shape[0], tn), col),
                  pl.BlockSpec((w3.shape[0], tn), col),
                  pl.BlockSpec((tm, tn), lambda i, j: (i, j)),
                  _mod_spec(which_gate, rows_per_mod, tm, tn)],
        out_specs=pl.BlockSpec((tm, tn), lambda i, j: (i, j)),
        compiler_params=_cparams(("parallel", "arbitrary")),
        name="w_out",
    )(a1, a2, a3, w1, w2, w3, x, mods)


def _gu_pack_kernel(w_ref, p_ref, o_ref):
    o_ref[...] = _dot(w_ref[...], p_ref[...]).astype(o_ref.dtype)


def _moe_gu_pack(w_gu, layer):
    _, e, dm, n = w_gu.shape
    half = n // 2
    col = jnp.arange(n)
    src = jnp.where(col < half, 2 * col, 2 * (col - half) + 1)
    perm = (jnp.arange(n)[:, None] == src[None, :]).astype(BF16)
    tk = _tile(dm, 1024)
    return pl.pallas_call(
        _gu_pack_kernel,
        out_shape=jax.ShapeDtypeStruct((e, dm, n), BF16),
        grid=(e, dm // tk),
        in_specs=[pl.BlockSpec((None, None, tk, n), lambda ei, i: (layer, ei, i, 0)),
                  pl.BlockSpec((n, n), lambda ei, i: (0, 0))],
        out_specs=pl.BlockSpec((None, tk, n), lambda ei, i: (ei, i, 0)),
        compiler_params=_cparams(("parallel", "arbitrary")),
        name="moe_gu_pack",
    )(w_gu, perm)


def _moe_kernel(te_ref, tv_ref, x_ref, wgl_ref, bg_ref, bl_ref, wd_ref, bd_ref, gate_ref, y_ref):
    i = pl.program_id(0)

    @pl.when(tv_ref[i] > 0)
    def _():
        gu = _dot(x_ref[...], wgl_ref[...])
        x_glu = jnp.minimum(gu[:, :EXPERT_FF] + bg_ref[...], SWIGLU_LIMIT)
        x_lin = jnp.clip(gu[:, EXPERT_FF:] + bl_ref[...], -SWIGLU_LIMIT, SWIGLU_LIMIT)
        act = (x_lin + 1.0) * x_glu * jax.nn.sigmoid(SWIGLU_ALPHA * x_glu)
        y = _dot(act, wd_ref[...]) + bd_ref[...]
        y_ref[...] = (gate_ref[:, 0:1] * y).astype(y_ref.dtype)

    @pl.when(tv_ref[i] == 0)
    def _():
        y_ref[...] = jnp.zeros(y_ref.shape, y_ref.dtype)


def _moe_experts(xs, gate_rows, tile_expert, tile_valid, w_gl, b_g, b_l, w_dn, b_dn, tm):
    rows = xs.shape[0]
    ff = EXPERT_FF
    grid_spec = pltpu.PrefetchScalarGridSpec(
        num_scalar_prefetch=2,
        grid=(rows // tm,),
        in_specs=[pl.BlockSpec((tm, D_MODEL), lambda i, te, tv: (i, 0)),
                  pl.BlockSpec((None, D_MODEL, 2 * ff), lambda i, te, tv: (te[i], 0, 0)),
                  pl.BlockSpec((None, 1, ff), lambda i, te, tv: (te[i], 0, 0)),
                  pl.BlockSpec((None, 1, ff), lambda i, te, tv: (te[i], 0, 0)),
                  pl.BlockSpec((None, ff, D_MODEL), lambda i, te, tv: (te[i], 0, 0)),
                  pl.BlockSpec((None, 1, D_MODEL), lambda i, te, tv: (te[i], 0, 0)),
                  pl.BlockSpec((tm, LANES), lambda i, te, tv: (i, 0))],
        out_specs=pl.BlockSpec((tm, D_MODEL), lambda i, te, tv: (i, 0)),
    )
    return pl.pallas_call(
        _moe_kernel,
        out_shape=jax.ShapeDtypeStruct((rows, D_MODEL), BF16),
        grid_spec=grid_spec,
        compiler_params=_cparams(("arbitrary",)),
        name="moe_experts",
    )(tile_expert, tile_valid, xs, w_gl, b_g, b_l, w_dn, b_dn, gate_rows)


def _moe_combine_kernel(y_ref, x_ref, g_ref, o_ref):
    acc = y_ref[0].astype(F32)
    for kk in range(1, TOP_K):
        acc = acc + y_ref[kk].astype(F32)
    o_ref[...] = x_ref[...] + g_ref[...] * acc


def _moe_combine(y4, x, mods, which_gate, rows_per_mod, row_off):
    m = x.shape[0]
    tm = _tile(math.gcd(math.gcd(m, rows_per_mod), row_off), 256)
    off = row_off // tm
    return pl.pallas_call(
        _moe_combine_kernel,
        out_shape=jax.ShapeDtypeStruct((m, D_MODEL), F32),
        grid=(m // tm,),
        in_specs=[pl.BlockSpec((TOP_K, tm, D_MODEL), lambda i: (0, off + i, 0)),
                  pl.BlockSpec((tm, D_MODEL), lambda i: (i, 0)),
                  _mod_spec(which_gate, rows_per_mod, tm)],
        out_specs=pl.BlockSpec((tm, D_MODEL), lambda i: (i, 0)),
        compiler_params=_cparams(("parallel",)),
        name="moe_combine",
    )(y4, x, mods)


def _moe_dispatch(idx, wts, tm):
    t = idx.shape[0]
    n = t * TOP_K
    e_flat = idx[:, :TOP_K].reshape(n)
    w_flat = wts[:, :TOP_K].reshape(n)
    order = jnp.argsort(e_flat, stable=True).astype(jnp.int32)
    e_sorted = e_flat[order]
    bounds = jnp.searchsorted(e_sorted, jnp.arange(N_EXPERTS + 1, dtype=jnp.int32), side="left").astype(jnp.int32)
    start = bounds[:-1]
    counts = bounds[1:] - start
    padded = ((counts + tm - 1) // tm) * tm
    ends = jnp.cumsum(padded)
    start_p = ends - padded
    n_rows = ((n + N_EXPERTS * (tm - 1)) // tm + 1) * tm
    tile_start = jnp.arange(n_rows // tm, dtype=jnp.int32) * tm
    tile_expert = jnp.minimum(jnp.searchsorted(ends, tile_start, side="right"), N_EXPERTS - 1).astype(jnp.int32)
    tile_valid = (tile_start < ends[-1]).astype(jnp.int32)
    rows = jnp.arange(n_rows, dtype=jnp.int32)
    row_expert = jnp.repeat(tile_expert, tm)
    off = rows - start_p[row_expert]
    valid = (off < counts[row_expert]) & (rows < ends[-1])
    assign = order[jnp.where(valid, start[row_expert] + off, 0)]
    row_token = jnp.where(valid, assign // TOP_K, 0)
    row_gate = jnp.where(valid, w_flat[assign], 0.0)
    rank = jnp.arange(n, dtype=jnp.int32) - start[e_sorted]
    dest = start_p[e_sorted] + rank
    pos = dest[jnp.argsort(order)].reshape(t, TOP_K)
    return row_token, row_gate, pos, tile_expert, tile_valid


def _moe(h2_list, idx_list, wt_list, lw):
    tm = 256
    h2 = jnp.concatenate(h2_list, axis=0) if len(h2_list) > 1 else h2_list[0]
    idx = jnp.concatenate(idx_list, axis=0) if len(idx_list) > 1 else idx_list[0]
    wts = jnp.concatenate(wt_list, axis=0) if len(wt_list) > 1 else wt_list[0]
    row_token, row_gate, pos, tile_expert, tile_valid = _moe_dispatch(idx, wts, tm)
    xs = jnp.take(h2, row_token, axis=0, mode="clip")
    gate_rows = jnp.broadcast_to(row_gate[:, None], (row_gate.shape[0], LANES))
    ys = _moe_experts(xs, gate_rows, tile_expert, tile_valid, lw["moe_w_gl"],
                      lw["moe_b_g"], lw["moe_b_l"], lw["moe_w_dn"], lw["moe_b_dn"], tm)
    return jnp.take(ys, pos.T, axis=0, mode="clip")


def _prep_layer(lp):
    w = {}
    w_in = lp["w_in"]
    e_gdn = GDN_NQKV + GDN_VW + 4 * GDN_HEADS
    e_mla = e_gdn + MLA_Q_RANK + MLA_KV_RANK + MLA_ROPE
    o_bg = GDN_NQKV + GDN_VW
    zeros = lambda n: jnp.zeros((D_MODEL, n), w_in.dtype)
    w["w_in"] = jnp.concatenate([
        w_in[:, :o_bg],
        w_in[:, e_gdn:e_gdn + MLA_Q_RANK],
        w_in[:, e_mla:],
        w_in[:, e_gdn + MLA_Q_RANK:e_gdn + MLA_Q_RANK + MLA_KV_RANK],
        w_in[:, o_bg:e_gdn], zeros(LANES - 4 * GDN_HEADS),
        w_in[:, e_mla - MLA_ROPE:e_mla], zeros(LANES - MLA_ROPE),
    ], axis=1).astype(BF16)
    uq = lp["mla_w_uq"].reshape(MLA_Q_RANK, MLA_HEADS, MLA_NOPE + MLA_ROPE)
    uq = jnp.pad(uq, ((0, 0), (0, 0), (0, MLA_QP - MLA_NOPE - MLA_ROPE)))
    w["w_uq"] = uq.reshape(MLA_Q_RANK, MLA_HEADS * MLA_QP).astype(BF16)
    w["w_ukv"] = lp["mla_w_ukv"].astype(BF16)
    pad_r = lambda g: jnp.pad(g.astype(F32), (0, LANES - MLA_ROPE)).reshape(1, LANES)
    w["mla_gains"] = (lp["mla_qn_norm"].astype(F32).reshape(1, LANES), pad_r(lp["mla_qr_norm"]),
                      lp["mla_kn_norm"].astype(F32).reshape(1, LANES), pad_r(lp["mla_kr_norm"]))
    w["s5_w"], w["s5_a"], w["s5_c"] = _s5_params(lp)
    w["s5_w_glu"] = lp["s5_w_glu"].astype(BF16)
    wo = lp["w_out"].astype(BF16)
    w["w_out"] = (wo[:GDN_VW], wo[GDN_VW:GDN_VW + MLA_HEADS * MLA_V], wo[GDN_VW + MLA_HEADS * MLA_V:])
    w["w_router"] = jnp.pad(lp["moe_w_router"].astype(F32), ((0, 0), (0, LANES - N_EXPERTS)))
    w["b_router"] = jnp.pad(lp["moe_b_router"].astype(F32), (0, LANES - N_EXPERTS)).reshape(1, LANES)
    w["moe_w_gl"] = _moe_gu_pack(lp["moe_w_gu_stacked"], lp["layer"])
    w["moe_b_g"] = lp["moe_b_gu"][:, None, 0::2].astype(F32)
    w["moe_b_l"] = lp["moe_b_gu"][:, None, 1::2].astype(F32)
    w["moe_w_dn"] = lp["moe_w_dn"].astype(BF16)
    w["moe_b_dn"] = lp["moe_b_dn"][:, None, :].astype(F32)
    return w


def _rope_tables(n_tokens):
    n_rows = n_tokens // GRID_W
    row = jnp.repeat(jnp.arange(n_rows), GRID_W).astype(F32)
    col = jnp.tile(jnp.arange(GRID_W), n_rows).astype(F32)
    axis_dim = MLA_ROPE // 2
    inv = ROPE_THETA ** (-jnp.arange(0, axis_dim, 2, dtype=F32) / axis_dim)
    ang = jnp.concatenate([row[:, None] * inv, col[:, None] * inv], axis=-1)
    cos, sin = jnp.cos(ang), jnp.sin(ang)
    pad = jnp.zeros((n_tokens, LANES - MLA_ROPE), F32)
    return (jnp.concatenate([cos, cos, pad], axis=1), jnp.concatenate([-sin, sin, pad], axis=1))


def _mixers(streams, lp, lw, rope_tabs, batch, ctx_out):
    (p_c, n_c), (p_l, n_l) = streams

    gdn_state = jnp.zeros((2, batch, GDN_HEADS, GDN_DK, GDN_DV), F32)
    gdn = []
    for p, seg in streams:
        qkv = _gdn_prep(p, lp["gdn_conv"].astype(F32), batch, seg)
        scal, scal_t = _gdn_scalars(p, lp["gdn_a_log"], lp["gdn_dt_bias"])
        o2, gdn_state = _gdn_scan(qkv, scal, scal_t, gdn_state, batch, seg)
        gdn.append(_gdn_out(o2, p, lp["gdn_norm"].astype(F32)) if (ctx_out or p is p_l) else None)

    heads = []
    for (p, seg), rope in zip(streams, (False, True)):
        q = _mm(p, lw["w_uq"], gain=lp["mla_q_norm"], prologue="rms", a_colblk=COL_MQ // MLA_Q_RANK,
                tn=1536, name="mla_uq")
        kv = _mm(p, lw["w_ukv"], gain=lp["mla_kv_norm"], prologue="rms", a_colblk=COL_MKV // MLA_KV_RANK,
                 tn=1536, name="mla_ukv")
        heads.append(_mla_prep(q, kv, p, rope_tabs[0], rope_tabs[1], lw["mla_gains"], seg, rope))
    (qh_c, kh_c, vh_c), (qh_l, kh_l, vh_l) = heads
    mla_l = _attention(qh_l, kh_c, vh_c, kh_l, vh_l, batch, n_l, n_c, n_l)
    mla_c = _attention(qh_c, kh_c, vh_c, None, None, batch, n_c, n_c, 0) if ctx_out else None

    s5_state = jnp.zeros((2, S5_SLABS, batch, 2 * S5_SLAB_STATE), F32)
    s5 = []
    for p, seg in streams:
        u_t = jnp.transpose(p[:, COL_S5:COL_S5 + S5_WIDTH].reshape(batch, seg, S5_WIDTH), (1, 0, 2))
        y2, s5_state = _s5_scan(u_t, lw["s5_w"], lw["s5_a"], lw["s5_c"], s5_state)
        if ctx_out or p is p_l:
            s5.append(_s5_glu(y2.reshape(2, seg, batch * S5_WIDTH), p, lp["s5_d"].astype(F32),
                              lw["s5_w_glu"], lp["s5_b_glu"].astype(F32), batch, seg))
        else:
            s5.append(None)
    return (gdn[0], mla_c, s5[0]), (gdn[1], mla_l, s5[1])


def _layer(x, xc, mods_l, mods_c, lp, rope_tabs, batch, n_l, n_c, last):
    lw = _prep_layer(lp)
    ctx_out = not last
    rows_c = batch * n_c
    h_l = _norm_mod(x, lp["norm1"], mods_l, 0, 1, n_l)
    h_c = _norm_mod(xc, lp["norm1"], mods_c, 0, 1, rows_c)
    p_l = _mm(h_l, lw["w_in"], tn=1280, name="w_in")
    p_c = _mm(h_c, lw["w_in"], tn=1280, name="w_in")
    mix_c, mix_l = _mixers([(p_c, n_c), (p_l, n_l)], lp, lw, rope_tabs, batch, ctx_out)
    x = _wout(*mix_l, *lw["w_out"], x, mods_l, 2, n_l)
    h2_l, idx_l, wt_l = _norm_route(x, lp["norm2"], mods_l, 3, 4, n_l, lw["w_router"], lw["b_router"])
    if last:
        y4 = _moe([h2_l], [idx_l], [wt_l], lw)
        return _moe_combine(y4, x, mods_l, 5, n_l, 0), xc
    xc = _wout(*mix_c, *lw["w_out"], xc, mods_c, 2, rows_c)
    h2_c, idx_c, wt_c = _norm_route(xc, lp["norm2"], mods_c, 3, 4, rows_c, lw["w_router"], lw["b_router"])
    y4 = _moe([h2_c, h2_l], [idx_c, idx_l], [wt_c, wt_l], lw)
    x_new = _moe_combine(y4, x, mods_l, 5, n_l, rows_c)
    xc_new = _moe_combine(y4, xc, mods_c, 5, rows_c, 0)
    return x_new, xc_new


def _ada_mods(cvecs, lp):
    r = cvecs.shape[0]
    rp = -(-r // SUBLANES) * SUBLANES
    cv = jnp.pad(cvecs.astype(F32), ((0, rp - r), (0, 0)))
    t = _mm(cv, lp["ada_dn"].astype(F32), prologue="silu", exact=True, name="ada_dn")
    m = _mm(t, lp["ada_up"].astype(F32), bias=lp["ada_b"], exact=True, tn=2048, name="ada_up")
    return m[:r].reshape(r * N_MOD, 1, D_MODEL)


def kernel(x, c, ctx, c_ctx, ada_dn, ada_up, ada_b, norm1, norm2, w_in, gdn_conv, gdn_a_log, gdn_dt_bias, gdn_norm, mla_q_norm, mla_kv_norm, mla_w_uq, mla_w_ukv, mla_qn_norm, mla_qr_norm, mla_kn_norm, mla_kr_norm, s5_lam_re, s5_lam_im, s5_log_step, s5_b_re, s5_b_im, s5_c_re, s5_c_im, s5_d, s5_w_glu, s5_b_glu, w_out, moe_w_router, moe_b_router, moe_w_gu, moe_b_gu, moe_w_dn, moe_b_dn):
    batch, n_l, _ = x.shape
    n_c = ctx.shape[1]
    depth = w_in.shape[0]
    params = dict(ada_dn=ada_dn, ada_up=ada_up, ada_b=ada_b, norm1=norm1, norm2=norm2, w_in=w_in,
                  gdn_conv=gdn_conv, gdn_a_log=gdn_a_log, gdn_dt_bias=gdn_dt_bias, gdn_norm=gdn_norm,
                  mla_q_norm=mla_q_norm, mla_kv_norm=mla_kv_norm, mla_w_uq=mla_w_uq, mla_w_ukv=mla_w_ukv,
                  mla_qn_norm=mla_qn_norm, mla_qr_norm=mla_qr_norm, mla_kn_norm=mla_kn_norm,
                  mla_kr_norm=mla_kr_norm, s5_lam_re=s5_lam_re, s5_lam_im=s5_lam_im,
                  s5_log_step=s5_log_step, s5_b_re=s5_b_re, s5_b_im=s5_b_im, s5_c_re=s5_c_re,
                  s5_c_im=s5_c_im, s5_d=s5_d, s5_w_glu=s5_w_glu, s5_b_glu=s5_b_glu, w_out=w_out,
                  moe_w_router=moe_w_router, moe_b_router=moe_b_router, moe_w_gu=moe_w_gu,
                  moe_b_gu=moe_b_gu, moe_w_dn=moe_w_dn, moe_b_dn=moe_b_dn)
    rope_tabs = _rope_tables(n_l)
    xl = x.reshape(batch * n_l, D_MODEL).astype(F32)
    xc = ctx.reshape(batch * n_c, D_MODEL).astype(F32)
    cvecs = jnp.concatenate([c.astype(F32), c_ctx.astype(F32)[None, :]], axis=0)
    for layer in range(depth):
        lp = {name: val[layer] for name, val in params.items() if name != "moe_w_gu"}
        lp["moe_w_gu_stacked"], lp["layer"] = moe_w_gu, layer
        mods = _ada_mods(cvecs, lp)
        xl, xc = _layer(xl, xc, mods[:batch * N_MOD], mods[batch * N_MOD:], lp, rope_tabs, batch, n_l, n_c,
                        layer == depth - 1)
    return xl.reshape(batch, n_l, D_MODEL).astype(x.dtype)
```
